```python
import jax, jax.numpy as jnp
from jax import lax
import numpy as np

D_MODEL = 1024
BATCH = 8
SEQ = 4096
DEPTH = 2

CHUNK = 128
N_SGU_GROUPS = 8
SGU_WIDTH = D_MODEL
SGU_GROUP_DIM = SGU_WIDTH // N_SGU_GROUPS
N_HEADS = 8
HEAD_DIM = 128
ATTN_WIDTH = N_HEADS * HEAD_DIM
Q_BLOCK = 128
D_FF = -(-8 * D_MODEL // (3 * 256)) * 256
EPS = 1e-6

_IN_SIZES = (SGU_WIDTH, SGU_WIDTH, ATTN_WIDTH, ATTN_WIDTH, ATTN_WIDTH, D_MODEL, D_MODEL, N_HEADS)
IN_WIDTH = sum(_IN_SIZES)
IN_SPLITS = tuple(int(s) for s in np.cumsum(_IN_SIZES)[:-1])

kernel_name = "hybrid_gmlp_fox_gated_block"


def rmsnorm(x, g):
    xf = x.astype(jnp.float32)
    r = lax.rsqrt(jnp.mean(xf * xf, axis=-1, keepdims=True) + EPS)
    return (xf * r * g.astype(jnp.float32)).astype(x.dtype)


def layernorm(x, g):
    xf = x.astype(jnp.float32)
    mu = jnp.mean(xf, axis=-1, keepdims=True)
    xc = xf - mu
    r = lax.rsqrt(jnp.mean(xc * xc, axis=-1, keepdims=True) + EPS)
    return (xc * r * g.astype(jnp.float32)).astype(x.dtype)


def spatial_gating(u, v, w_s, b_s, g_v):
    bsz, seq = v.shape[0], v.shape[1]
    v = layernorm(v, g_v)
    vc = v.reshape(bsz, seq // CHUNK, CHUNK, N_SGU_GROUPS, SGU_GROUP_DIM)
    causal = jnp.tril(jnp.ones((CHUNK, CHUNK), dtype=bool))
    w = jnp.where(causal[None], w_s, jnp.zeros_like(w_s))
    mixed = jnp.einsum('gts,bcsgd->bctgd', w, vc) + b_s.T[None, None, :, :, None]
    return u * mixed.reshape(bsz, seq, SGU_WIDTH)


def forgetting_attention(q, k, v, f_logit, b_f):
    bsz, seq = q.shape[0], q.shape[1]

    def heads(t):
        return t.reshape(bsz, seq, N_HEADS, HEAD_DIM).transpose(0, 2, 1, 3)

    q = heads(q) * (HEAD_DIM ** -0.5)
    k = heads(k)
    v = heads(v)
    log_f = jax.nn.log_sigmoid((f_logit + b_f).astype(jnp.float32))
    c = jnp.cumsum(log_f, axis=1).transpose(0, 2, 1)
    k_pos = jnp.arange(seq)

    def block(i):
        start = i * Q_BLOCK
        qi = lax.dynamic_slice_in_dim(q, start, Q_BLOCK, axis=2)
        ci = lax.dynamic_slice_in_dim(c, start, Q_BLOCK, axis=2)
        s = jnp.einsum('bhqd,bhkd->bhqk', qi, k).astype(jnp.float32)
        s = s + ci[..., None] - c[:, :, None, :]
        q_pos = start + jnp.arange(Q_BLOCK)
        s = jnp.where(k_pos[None, :] <= q_pos[:, None], s, -jnp.inf)
        p = jax.nn.softmax(s, axis=-1).astype(v.dtype)
        return jnp.einsum('bhqk,bhkd->bhqd', p, v)

    out = lax.map(block, jnp.arange(seq // Q_BLOCK))
    return out.transpose(1, 0, 3, 2, 4).reshape(bsz, seq, ATTN_WIDTH)


def swiglu(h, w_gate, w_up, w_down):
    return (jax.nn.silu(h @ w_gate) * (h @ w_up)) @ w_down


def setup_inputs(seed: int = 0) -> dict:
    key = jax.random.key(seed)
    ks = jax.random.split(key, 15)
    f32 = jnp.float32

    def nrm(k, shape, scale):
        return jax.random.normal(k, shape, f32) * scale

    def gain(k, shape):
        return 1.0 + 0.1 * jax.random.normal(k, shape, f32)

    x = jax.random.normal(ks[0], (BATCH, SEQ, D_MODEL), f32)
    mix_pre_g = gain(ks[1], (DEPTH, D_MODEL))
    w_in = nrm(ks[2], (DEPTH, D_MODEL, IN_WIDTH), D_MODEL ** -0.5)
    b_forget = jnp.linspace(1.0, 6.0, N_HEADS, dtype=f32)[None, :] + 0.1 * jax.random.normal(ks[3], (DEPTH, N_HEADS), f32)
    sgu_norm_g = gain(ks[4], (DEPTH, SGU_WIDTH))
    w_spatial = nrm(ks[5], (DEPTH, N_SGU_GROUPS, CHUNK, CHUNK), CHUNK ** -0.5)
    b_spatial = 1.0 + 0.1 * jax.random.normal(ks[6], (DEPTH, N_SGU_GROUPS, CHUNK), f32)
    w_out = nrm(ks[7], (DEPTH, D_MODEL, D_MODEL), D_MODEL ** -0.5)
    mix_post_g = gain(ks[8], (DEPTH, D_MODEL))
    ffn_pre_g = gain(ks[9], (DEPTH, D_MODEL))
    w_gate = nrm(ks[10], (DEPTH, D_MODEL, D_FF), D_MODEL ** -0.5)
    w_up = nrm(ks[11], (DEPTH, D_MODEL, D_FF), D_MODEL ** -0.5)
    w_down = nrm(ks[12], (DEPTH, D_FF, D_MODEL), D_FF ** -0.5)
    ffn_post_g = gain(ks[13], (DEPTH, D_MODEL))
    return {"x": x, "mix_pre_g": mix_pre_g, "w_in": w_in, "b_forget": b_forget,
            "sgu_norm_g": sgu_norm_g, "w_spatial": w_spatial, "b_spatial": b_spatial,
            "w_out": w_out, "mix_post_g": mix_post_g, "ffn_pre_g": ffn_pre_g,
            "w_gate": w_gate, "w_up": w_up, "w_down": w_down, "ffn_post_g": ffn_post_g}


def reference(x, mix_pre_g, w_in, b_forget, sgu_norm_g, w_spatial, b_spatial, w_out,
              mix_post_g, ffn_pre_g, w_gate, w_up, w_down, ffn_post_g):
    for l in range(DEPTH):
        h = rmsnorm(x, mix_pre_g[l])
        proj = h @ w_in[l]
        u, v_s, q, k, v_a, g_a, g_b, f_logit = jnp.split(proj, IN_SPLITS, axis=-1)
        y_a = spatial_gating(jax.nn.gelu(u), jax.nn.gelu(v_s), w_spatial[l], b_spatial[l], sgu_norm_g[l])
        y_b = forgetting_attention(q, k, v_a, f_logit, b_forget[l])
        merged = jax.nn.sigmoid(g_a) * y_a + jax.nn.sigmoid(g_b) * y_b
        x = x + rmsnorm(merged @ w_out[l], mix_post_g[l])
        h = rmsnorm(x, ffn_pre_g[l])
        x = x + rmsnorm(swiglu(h, w_gate[l], w_up[l], w_down[l]), ffn_post_g[l])
    return x
```

```python
import functools
import math

import jax
import jax.numpy as jnp
from jax import lax
from jax.experimental import pallas as pl
from jax.experimental.pallas import tpu as pltpu

F32 = jnp.float32
BF16 = jnp.bfloat16

EPS = 1e-6
LANES = 128
CHUNK = 128
N_GROUPS = 8
N_HEADS = 8
HEAD_DIM = 128
LOG2E = math.log2(math.e)
NEG_BIG = -1e30
VMEM_LIMIT = 56 * 1024 * 1024

TOKEN_TILE = 256
Q_TILE = 512
KV_TILE = 512


def _gelu(x):
    c = math.sqrt(2.0 / math.pi)
    return 0.5 * x * (1.0 + jnp.tanh(c * (x + 0.044715 * (x * x * x))))


def _sigmoid(x):
    return 0.5 * (1.0 + jnp.tanh(0.5 * x))


def _log_sigmoid(x):
    return jnp.minimum(x, 0.0) - jnp.log1p(jnp.exp(-jnp.abs(x)))


def _rms_scale(x):
    return lax.rsqrt(jnp.mean(x * x, axis=-1, keepdims=True) + EPS)


def _resident(shape):
    nd = len(shape)
    return pl.BlockSpec(shape, lambda *_: (0,) * nd, pipeline_mode=pl.Buffered(1))


def _inproj_kernel(x_ref, gpre_ref, w_ref, wf_ref, bf_ref, gv_ref, wsp_ref, bsp_ref,
                   ya_ref, q_ref, k_ref, v_ref, sgb_ref, c_ref, carry_ref,
                   *, tiles_per_seq, d_model):
    tm = x_ref.shape[0]
    x = x_ref[...]
    h = (x * _rms_scale(x) * gpre_ref[...]).astype(BF16)

    def proj(seg):
        return jnp.dot(h, w_ref[:, seg * d_model:(seg + 1) * d_model],
                       preferred_element_type=F32)

    gu = _gelu(proj(0))
    gv = _gelu(proj(1))
    gc = gv - jnp.mean(gv, axis=-1, keepdims=True)
    vn = (gc * lax.rsqrt(jnp.mean(gc * gc, axis=-1, keepdims=True) + EPS)
          * gv_ref[...]).astype(BF16)
    sga = _sigmoid(proj(5))
    row = lax.broadcasted_iota(jnp.int32, (CHUNK, CHUNK), 0)
    col = lax.broadcasted_iota(jnp.int32, (CHUNK, CHUNK), 1)
    causal = col <= row
    for g in range(N_GROUPS):
        wg = jnp.where(causal, wsp_ref[g], 0.0).astype(BF16)
        bcol = bsp_ref[:, g:g + 1]
        cs = slice(g * LANES, (g + 1) * LANES)
        for c in range(tm // CHUNK):
            rs = slice(c * CHUNK, (c + 1) * CHUNK)
            mixed = jnp.dot(wg, vn[rs, cs], preferred_element_type=F32) + bcol
            ya_ref[rs, cs] = (sga[rs, cs] * gu[rs, cs] * mixed).astype(BF16)

    q_ref[...] = (proj(2) * (HEAD_DIM ** -0.5 * LOG2E)).astype(BF16)
    k_ref[...] = proj(3).astype(BF16)
    v_ref[...] = proj(4).astype(BF16)
    sgb_ref[...] = _sigmoid(proj(6)).astype(BF16)

    @pl.when(pl.program_id(0) % tiles_per_seq == 0)
    def _():
        carry_ref[...] = jnp.zeros_like(carry_ref)

    f = jnp.dot(h, wf_ref[...], preferred_element_type=F32) + bf_ref[...]
    logf = _log_sigmoid(f)
    hi = logf.astype(BF16)
    r1 = logf - hi.astype(F32)
    mid = r1.astype(BF16)
    lo = (r1 - mid.astype(F32)).astype(BF16)
    ti = lax.broadcasted_iota(jnp.int32, (tm, tm), 0)
    tj = lax.broadcasted_iota(jnp.int32, (tm, tm), 1)
    tri = jnp.where(tj <= ti, 1.0, 0.0).astype(BF16)
    csum = (jnp.dot(tri, hi, preferred_element_type=F32)
            + jnp.dot(tri, mid, preferred_element_type=F32)
            + jnp.dot(tri, lo, preferred_element_type=F32)) + carry_ref[...]
    carry_ref[...] = csum[tm - 1:tm, :]
    c_ref[...] = csum * LOG2E


def _inproj(x2, gpre, w_main, w_f, b_f, g_v, w_sp, b_sp_t, *, seq):
    n_tok, d_model = x2.shape
    tm = TOKEN_TILE
    tok = lambda i: (i, 0)
    act = pl.BlockSpec((tm, d_model), tok)
    out_bf = jax.ShapeDtypeStruct((n_tok, d_model), BF16)
    kern = functools.partial(_inproj_kernel, tiles_per_seq=seq // tm, d_model=d_model)
    return pl.pallas_call(
        kern,
        grid=(n_tok // tm,),
        in_specs=[act, _resident(gpre.shape), _resident(w_main.shape), _resident(w_f.shape),
                  _resident(b_f.shape), _resident(g_v.shape), _resident(w_sp.shape),
                  _resident(b_sp_t.shape)],
        out_specs=[act, act, act, act, act, pl.BlockSpec((tm, LANES), tok)],
        out_shape=[out_bf, out_bf, out_bf, out_bf, out_bf,
                   jax.ShapeDtypeStruct((n_tok, LANES), F32)],
        scratch_shapes=[pltpu.VMEM((1, LANES), F32)],
        compiler_params=pltpu.CompilerParams(dimension_semantics=("arbitrary",),
                                             vmem_limit_bytes=VMEM_LIMIT),
        name="inproj_sgu",
    )(x2, gpre, w_main, w_f, b_f, g_v, w_sp, b_sp_t)


def _attn_kernel(q_ref, k_ref, v_ref, ccol_ref, crow_ref, o_ref):
    tq = q_ref.shape[0]
    tk = KV_TILE
    head = pl.program_id(1)
    i = pl.program_id(2)
    q = q_ref[...]
    lane = lax.broadcasted_iota(jnp.int32, ccol_ref.shape, 1)
    ci = jnp.sum(jnp.where(lane == head, ccol_ref[...], 0.0), axis=1, keepdims=True)

    def step(j, carry, masked):
        m, l, acc = carry
        off = pl.multiple_of(j * tk, tk)
        kj = k_ref[pl.ds(off, tk), :]
        vj = v_ref[pl.ds(off, tk), :]
        s = lax.dot_general(q, kj, (((1,), (1,)), ((), ())), preferred_element_type=F32)
        s = s + (ci - crow_ref[:, pl.ds(off, tk)])
        if masked:
            r = lax.broadcasted_iota(jnp.int32, (tq, tk), 0)
            c = lax.broadcasted_iota(jnp.int32, (tq, tk), 1)
            s = jnp.where(c <= r, s, NEG_BIG)
        m_new = jnp.maximum(m, jnp.max(s, axis=1, keepdims=True))
        alpha = jnp.exp2(m - m_new)
        p = jnp.exp2(s - m_new)
        l = alpha * l + jnp.sum(p, axis=1, keepdims=True)
        acc = alpha * acc + jnp.dot(p.astype(BF16), vj, preferred_element_type=F32)
        return m_new, l, acc

    init = (jnp.full((tq, 1), NEG_BIG, F32), jnp.zeros((tq, 1), F32),
            jnp.zeros((tq, HEAD_DIM), F32))
    carry = lax.fori_loop(0, i, functools.partial(step, masked=False), init)
    m, l, acc = step(i, carry, masked=True)
    o_ref[...] = (acc / l).astype(BF16)


def _attention(q, k, v, c_col, c_row):
    bsz, seq, width = q.shape
    tq = Q_TILE
    assert tq == KV_TILE and seq % tq == 0
    return pl.pallas_call(
        _attn_kernel,
        grid=(bsz, N_HEADS, seq // tq),
        in_specs=[
            pl.BlockSpec((None, tq, HEAD_DIM), lambda b, h, i: (b, i, h)),
            pl.BlockSpec((None, seq, HEAD_DIM), lambda b, h, i: (b, 0, h)),
            pl.BlockSpec((None, seq, HEAD_DIM), lambda b, h, i: (b, 0, h)),
            pl.BlockSpec((None, tq, LANES), lambda b, h, i: (b, i, 0)),
            pl.BlockSpec((None, 1, seq), lambda b, h, i: (b * N_HEADS + h, 0, 0)),
        ],
        out_specs=pl.BlockSpec((None, tq, HEAD_DIM), lambda b, h, i: (b, i, h)),
        out_shape=jax.ShapeDtypeStruct((bsz, seq, width), BF16),
        compiler_params=pltpu.CompilerParams(
            dimension_semantics=("arbitrary", "arbitrary", "arbitrary"),
            vmem_limit_bytes=VMEM_LIMIT),
        name="fox_attention",
    )(q, k, v, c_col, c_row)


def _out_ffn_kernel(x_ref, ya_ref, yb_ref, sgb_ref, wo_ref, gpost_ref, gffn_ref,
                    wg_ref, wu_ref, wd_ref, gffn_post_ref, o_ref):
    merged = (ya_ref[...].astype(F32)
              + sgb_ref[...].astype(F32) * yb_ref[...].astype(F32)).astype(BF16)
    mix = jnp.dot(merged, wo_ref[...], preferred_element_type=F32)
    x1 = x_ref[...] + mix * _rms_scale(mix) * gpost_ref[...]
    h = (x1 * _rms_scale(x1) * gffn_ref[...]).astype(BF16)
    gate = jnp.dot(h, wg_ref[...], preferred_element_type=F32)
    up = jnp.dot(h, wu_ref[...], preferred_element_type=F32)
    act = (gate * _sigmoid(gate) * up).astype(BF16)
    down = jnp.dot(act, wd_ref[...], preferred_element_type=F32)
    o_ref[...] = x1 + down * _rms_scale(down) * gffn_post_ref[...]


def _out_ffn(x2, ya, yb, sgb, w_o, g_post, g_ffn, w_g, w_u, w_d, g_ffn_post):
    n_tok, d_model = x2.shape
    tm = TOKEN_TILE
    act = pl.BlockSpec((tm, d_model), lambda i: (i, 0))
    return pl.pallas_call(
        _out_ffn_kernel,
        grid=(n_tok // tm,),
        in_specs=[act, act, act, act, _resident(w_o.shape), _resident(g_post.shape),
                  _resident(g_ffn.shape), _resident(w_g.shape), _resident(w_u.shape),
                  _resident(w_d.shape), _resident(g_ffn_post.shape)],
        out_specs=act,
        out_shape=jax.ShapeDtypeStruct((n_tok, d_model), F32),
        compiler_params=pltpu.CompilerParams(dimension_semantics=("arbitrary",),
                                             vmem_limit_bytes=VMEM_LIMIT),
        name="out_ffn",
    )(x2, ya, yb, sgb, w_o, g_post, g_ffn, w_g, w_u, w_d, g_ffn_post)


def kernel(x, mix_pre_g, w_in, b_forget, sgu_norm_g, w_spatial, b_spatial, w_out, mix_post_g, ffn_pre_g, w_gate, w_up, w_down, ffn_post_g):
    bsz, seq, d_model = x.shape
    depth = w_in.shape[0]
    n_main = 7 * d_model
    assert seq % Q_TILE == 0 and seq % TOKEN_TILE == 0 and TOKEN_TILE % CHUNK == 0
    assert w_in.shape[2] == n_main + N_HEADS and d_model == N_HEADS * HEAD_DIM
    x2 = x.reshape(bsz * seq, d_model)
    row = lambda g: g.reshape(1, -1)
    for l in range(depth):
        w_main = w_in[l, :, :n_main].astype(BF16)
        w_f = jnp.pad(w_in[l, :, n_main:], ((0, 0), (0, LANES - N_HEADS))).astype(BF16)
        b_f = jnp.pad(b_forget[l], (0, LANES - N_HEADS)).reshape(1, LANES)
        ya, q, k, v, sgb, c = _inproj(
            x2, row(mix_pre_g[l]), w_main, w_f, b_f, row(sgu_norm_g[l]),
            w_spatial[l], b_spatial[l].T, seq=seq)
        c3 = c.reshape(bsz, seq, LANES)
        c_row = c3[:, :, :N_HEADS].transpose(0, 2, 1).reshape(bsz * N_HEADS, 1, seq)
        to3 = lambda t: t.reshape(bsz, seq, d_model)
        yb = _attention(to3(q), to3(k), to3(v), c3, c_row)
        x2 = _out_ffn(x2, ya, yb.reshape(bsz * seq, d_model), sgb,
                      w_out[l].astype(BF16), row(mix_post_g[l]), row(ffn_pre_g[l]),
                      w_gate[l].astype(BF16), w_up[l].astype(BF16),
                      w_down[l].astype(BF16), row(ffn_post_g[l]))
    return x2.reshape(bsz, seq, d_model)
```

```python
import functools
import math

import jax
import jax.numpy as jnp
from jax import lax
from jax.experimental import pallas as pl
from jax.experimental.pallas import tpu as pltpu

F32 = jnp.float32
BF16 = jnp.bfloat16

EPS = 1e-6
LANES = 128
CHUNK = 128
N_GROUPS = 8
N_HEADS = 8
HEAD_DIM = 128
LOG2E = math.log2(math.e)
NEG_BIG = -1e30
VMEM_LIMIT = 56 * 1024 * 1024

TOKEN_TILE = 256
ATTN_TILE = 512


def _gelu(x):
    c = math.sqrt(2.0 / math.pi)
    return 0.5 * x * (1.0 + jnp.tanh(c * (x + 0.044715 * (x * x * x))))


def _sigmoid(x):
    return 0.5 * (1.0 + jnp.tanh(0.5 * x))


def _log_sigmoid(x):
    return jnp.minimum(x, 0.0) - jnp.log1p(jnp.exp(-jnp.abs(x)))


def _rms_scale(x):
    return lax.rsqrt(jnp.mean(x * x, axis=-1, keepdims=True) + EPS)


def _resident(shape):
    nd = len(shape)
    return pl.BlockSpec(shape, lambda *_: (0,) * nd, pipeline_mode=pl.Buffered(1))


def _inproj_kernel(x_ref, gpre_ref, w_ref, wqv_ref, wf_ref, bf_ref, gv_ref, wsp_ref, bsp_ref,
                   ya_ref, qt_ref, k_ref, vt_ref, sgb_ref, nc_ref, carry_ref,
                   *, tiles_per_seq, d_model):
    tm = x_ref.shape[0]
    x = x_ref[...]
    h = (x * _rms_scale(x) * gpre_ref[...]).astype(BF16)

    def proj(seg):
        return jnp.dot(h, w_ref[:, seg * d_model:(seg + 1) * d_model],
                       preferred_element_type=F32)

    gu = _gelu(proj(0))
    gv = _gelu(proj(1))
    gc = gv - jnp.mean(gv, axis=-1, keepdims=True)
    vn = (gc * lax.rsqrt(jnp.mean(gc * gc, axis=-1, keepdims=True) + EPS)
          * gv_ref[...]).astype(BF16)
    sga = _sigmoid(proj(3))
    row = lax.broadcasted_iota(jnp.int32, (CHUNK, CHUNK), 0)
    col = lax.broadcasted_iota(jnp.int32, (CHUNK, CHUNK), 1)
    causal = col <= row
    for g in range(N_GROUPS):
        wg = jnp.where(causal, wsp_ref[g], 0.0).astype(BF16)
        bcol = bsp_ref[:, g:g + 1]
        cs = slice(g * LANES, (g + 1) * LANES)
        for c in range(tm // CHUNK):
            rs = slice(c * CHUNK, (c + 1) * CHUNK)
            mixed = jnp.dot(wg, vn[rs, cs], preferred_element_type=F32) + bcol
            ya_ref[rs, cs] = (sga[rs, cs] * gu[rs, cs] * mixed).astype(BF16)

    qvt = lax.dot_general(wqv_ref[...], h, (((1,), (1,)), ((), ())),
                          preferred_element_type=F32)
    qt_ref[...] = (qvt[:d_model] * (HEAD_DIM ** -0.5 * LOG2E)).astype(BF16)
    vt_ref[...] = qvt[d_model:].astype(BF16)
    kf = proj(2)
    for hd in range(N_HEADS):
        k_ref[hd] = kf[:, hd * HEAD_DIM:(hd + 1) * HEAD_DIM].astype(BF16)
    sgb_ref[...] = _sigmoid(proj(4)).astype(BF16)

    @pl.when(pl.program_id(0) % tiles_per_seq == 0)
    def _():
        carry_ref[...] = jnp.zeros_like(carry_ref)

    f = jnp.dot(h, wf_ref[...], preferred_element_type=F32) + bf_ref[...]
    logf = _log_sigmoid(f)
    hi = logf.astype(BF16)
    r1 = logf - hi.astype(F32)
    mid = r1.astype(BF16)
    lo = (r1 - mid.astype(F32)).astype(BF16)
    ti = lax.broadcasted_iota(jnp.int32, (tm, tm), 0)
    tj = lax.broadcasted_iota(jnp.int32, (tm, tm), 1)
    tri = jnp.where(tj <= ti, 1.0, 0.0).astype(BF16)
    csum = (jnp.dot(tri, hi, preferred_element_type=F32)
            + jnp.dot(tri, mid, preferred_element_type=F32)
            + jnp.dot(tri, lo, preferred_element_type=F32)) + carry_ref[...]
    carry_ref[...] = csum[tm - 1:tm, :]
    negc = csum * (-LOG2E)
    for hd in range(N_HEADS):
        nc_ref[hd] = jnp.broadcast_to(negc[:, hd:hd + 1], (tm, LANES))


def _inproj(x2, gpre, w_main, w_qvt, w_f, b_f, g_v, w_sp, b_sp_t, *, bsz, seq):
    n_tok, d_model = x2.shape
    tm = TOKEN_TILE
    tps = seq // tm
    tok = lambda t: (t, 0)
    feat = lambda t: (t // tps, 0, t % tps)
    head = lambda t: (t // tps, 0, t % tps, 0)
    act = pl.BlockSpec((tm, d_model), tok)
    feat_spec = pl.BlockSpec((None, d_model, tm), feat)
    head_spec = pl.BlockSpec((None, N_HEADS, tm, HEAD_DIM), head)
    tok_bf = jax.ShapeDtypeStruct((n_tok, d_model), BF16)
    feat_bf = jax.ShapeDtypeStruct((bsz, d_model, seq), BF16)
    kern = functools.partial(_inproj_kernel, tiles_per_seq=tps, d_model=d_model)
    return pl.pallas_call(
        kern,
        grid=(n_tok // tm,),
        in_specs=[act, _resident(gpre.shape), _resident(w_main.shape), _resident(w_qvt.shape),
                  _resident(w_f.shape), _resident(b_f.shape), _resident(g_v.shape),
                  _resident(w_sp.shape), _resident(b_sp_t.shape)],
        out_specs=[act, feat_spec, head_spec, feat_spec, act, head_spec],
        out_shape=[tok_bf, feat_bf,
                   jax.ShapeDtypeStruct((bsz, N_HEADS, seq, HEAD_DIM), BF16),
                   feat_bf, tok_bf,
                   jax.ShapeDtypeStruct((bsz, N_HEADS, seq, LANES), F32)],
        scratch_shapes=[pltpu.VMEM((1, LANES), F32)],
        compiler_params=pltpu.CompilerParams(dimension_semantics=("arbitrary",),
                                             vmem_limit_bytes=VMEM_LIMIT),
        name="inproj_sgu",
    )(x2, gpre, w_main, w_qvt, w_f, b_f, g_v, w_sp, b_sp_t)


def _attn_kernel(qt_ref, k_ref, vt_ref, nc_ref, o_ref,
                 s0, s1, cm0, cm1, p0, p1, al0, al1, m_all, l_all, acc_all, *, n_tiles):
    t = ATTN_TILE
    bufs = ((s0, cm0, p0, al0), (s1, cm1, p1, al1))

    def tile(idx):
        return pl.ds(idx * t, t) if isinstance(idx, int) else pl.ds(pl.multiple_of(idx * t, t), t)

    def produce(pair, slot, masked):
        i, j = pair
        s_ref, cm_ref, _, _ = bufs[slot]
        s = (jnp.dot(k_ref[tile(j), :], qt_ref[:, tile(i)], preferred_element_type=F32)
             + pltpu.repeat(nc_ref[tile(j), :], t // LANES, axis=1))
        if masked:
            key = lax.broadcasted_iota(jnp.int32, (t, t), 0)
            qry = lax.broadcasted_iota(jnp.int32, (t, t), 1)
            s = jnp.where(key <= qry, s, NEG_BIG)
        s_ref[...] = s
        cm_ref[...] = jnp.max(s, axis=0, keepdims=True)

    def softmax(pair, slot):
        i, _ = pair
        s_ref, cm_ref, p_ref, al_ref = bufs[slot]
        m_old = m_all[i]
        m_new = jnp.maximum(m_old, cm_ref[...])
        alpha = jnp.exp2(m_old - m_new)
        p = jnp.exp2(s_ref[...] - m_new)
        l_all[i] = alpha * l_all[i] + jnp.sum(p, axis=0, keepdims=True)
        m_all[i] = m_new
        al_ref[...] = alpha
        p_ref[...] = p.astype(BF16)

    def accumulate(pair, slot):
        i, j = pair
        _, _, p_ref, al_ref = bufs[slot]
        pv = jnp.dot(vt_ref[:, tile(j)], p_ref[...], preferred_element_type=F32)
        acc_all[i] = al_ref[...] * acc_all[i] + pv

    def step(prev, cur, nxt, slot, nxt_masked):
        accumulate(prev, 1 - slot)
        produce(nxt, 1 - slot, nxt_masked)
        softmax(cur, slot)

    def below_next(pair):
        i, j = pair
        wrap = j + 1 == i
        return (jnp.where(wrap, jnp.minimum(i + 1, n_tiles - 1), i), jnp.where(wrap, 0, j + 1))

    m_all[...] = jnp.full_like(m_all, NEG_BIG)
    l_all[...] = jnp.zeros_like(l_all)
    acc_all[...] = jnp.zeros_like(acc_all)
    al1[...] = jnp.ones_like(al1)
    p1[...] = jnp.zeros_like(p1)

    n_below = n_tiles * (n_tiles - 1) // 2
    assert n_tiles % 2 == 0 and n_tiles >= 4 and n_below % 2 == 0
    produce((0, 0), 0, True)

    def diag_body(u2, _):
        u = 2 * u2
        um = jnp.maximum(u - 1, 0)
        step((um, um), (u, u), (u + 1, u + 1), 0, True)
        step((u, u), (u + 1, u + 1), (u + 2, u + 2), 1, True)
        return 0

    lax.fori_loop(0, (n_tiles - 2) // 2, diag_body, 0)
    d = n_tiles - 1
    step((d - 2, d - 2), (d - 1, d - 1), (d, d), 0, True)
    step((d - 1, d - 1), (d, d), (1, 0), 1, False)

    def below_body(_, carry):
        prev, cur = carry
        nxt = below_next(cur)
        step(prev, cur, nxt, 0, False)
        nxt2 = below_next(nxt)
        step(cur, nxt, nxt2, 1, False)
        return nxt, nxt2

    i32 = lambda v: jnp.int32(v)
    last, _ = lax.fori_loop(0, n_below // 2, below_body,
                            ((i32(d), i32(d)), (i32(1), i32(0))))
    accumulate(last, 1)

    def finish(i, _):
        o_ref[:, tile(i)] = (acc_all[i] * (1.0 / l_all[i])).astype(BF16)
        return 0

    lax.fori_loop(0, n_tiles, finish, 0)


def _attention(qt, k, vt, negc):
    bsz, width, seq = qt.shape
    t = ATTN_TILE
    assert seq % t == 0
    n_tiles = seq // t
    per_head = lambda b, h: (b, h, 0, 0)
    feat = pl.BlockSpec((None, HEAD_DIM, seq), lambda b, h: (b, h, 0))
    kern = functools.partial(_attn_kernel, n_tiles=n_tiles)
    row = pltpu.VMEM((1, t), F32)
    return pl.pallas_call(
        kern,
        grid=(bsz, N_HEADS),
        in_specs=[feat, pl.BlockSpec((None, None, seq, HEAD_DIM), per_head),
                  feat, pl.BlockSpec((None, None, seq, LANES), per_head)],
        out_specs=feat,
        out_shape=jax.ShapeDtypeStruct((bsz, width, seq), BF16),
        scratch_shapes=[pltpu.VMEM((t, t), F32), pltpu.VMEM((t, t), F32), row, row,
                        pltpu.VMEM((t, t), BF16), pltpu.VMEM((t, t), BF16), row, row,
                        pltpu.VMEM((n_tiles, 1, t), F32), pltpu.VMEM((n_tiles, 1, t), F32),
                        pltpu.VMEM((n_tiles, HEAD_DIM, t), F32)],
        compiler_params=pltpu.CompilerParams(
            dimension_semantics=("arbitrary", "arbitrary"),
            vmem_limit_bytes=VMEM_LIMIT),
        name="fox_attention",
    )(qt, k, vt, negc)


def _out_ffn_kernel(x_ref, ya_ref, ybt_ref, sgb_ref, wo_ref, gpost_ref, gffn_ref,
                    wg_ref, wu_ref, wd_ref, gffn_post_ref, o_ref):
    yb = ybt_ref[...].astype(F32).T
    merged = (ya_ref[...].astype(F32) + sgb_ref[...].astype(F32) * yb).astype(BF16)
    mix = jnp.dot(merged, wo_ref[...], preferred_element_type=F32)
    x1 = x_ref[...] + mix * _rms_scale(mix) * gpost_ref[...]
    h = (x1 * _rms_scale(x1) * gffn_ref[...]).astype(BF16)
    gate = jnp.dot(h, wg_ref[...], preferred_element_type=F32)
    up = jnp.dot(h, wu_ref[...], preferred_element_type=F32)
    act = (gate * _sigmoid(gate) * up).astype(BF16)
    down = jnp.dot(act, wd_ref[...], preferred_element_type=F32)
    o_ref[...] = x1 + down * _rms_scale(down) * gffn_post_ref[...]


def _out_ffn(x2, ya, ybt, sgb, w_o, g_post, g_ffn, w_g, w_u, w_d, g_ffn_post, *, seq):
    n_tok, d_model = x2.shape
    tm = TOKEN_TILE
    tps = seq // tm
    act = pl.BlockSpec((tm, d_model), lambda i: (i, 0))
    feat = pl.BlockSpec((None, d_model, tm), lambda i: (i // tps, 0, i % tps))
    return pl.pallas_call(
        _out_ffn_kernel,
        grid=(n_tok // tm,),
        in_specs=[act, act, feat, act, _resident(w_o.shape), _resident(g_post.shape),
                  _resident(g_ffn.shape), _resident(w_g.shape), _resident(w_u.shape),
                  _resident(w_d.shape), _resident(g_ffn_post.shape)],
        out_specs=act,
        out_shape=jax.ShapeDtypeStruct((n_tok, d_model), F32),
        compiler_params=pltpu.CompilerParams(dimension_semantics=("arbitrary",),
                                             vmem_limit_bytes=VMEM_LIMIT),
        name="out_ffn",
    )(x2, ya, ybt, sgb, w_o, g_post, g_ffn, w_g, w_u, w_d, g_ffn_post)


def kernel(x, mix_pre_g, w_in, b_forget, sgu_norm_g, w_spatial, b_spatial, w_out, mix_post_g, ffn_pre_g, w_gate, w_up, w_down, ffn_post_g):
    bsz, seq, d_model = x.shape
    depth = w_in.shape[0]
    n_main = 7 * d_model
    assert seq % ATTN_TILE == 0 and seq % TOKEN_TILE == 0 and TOKEN_TILE % CHUNK == 0
    assert w_in.shape[2] == n_main + N_HEADS and d_model == N_HEADS * HEAD_DIM
    x2 = x.reshape(bsz * seq, d_model)
    row = lambda g: g.reshape(1, -1)
    seg = lambda w, s: w[:, s * d_model:(s + 1) * d_model]
    for l in range(depth):
        w = w_in[l]
        w_main = jnp.concatenate([seg(w, s) for s in (0, 1, 3, 5, 6)], axis=1).astype(BF16)
        w_qvt = jnp.concatenate([seg(w, 2).T, seg(w, 4).T], axis=0).astype(BF16)
        w_f = jnp.pad(w[:, n_main:], ((0, 0), (0, LANES - N_HEADS))).astype(BF16)
        b_f = jnp.pad(b_forget[l], (0, LANES - N_HEADS)).reshape(1, LANES)
        ya, qt, k, vt, sgb, negc = _inproj(
            x2, row(mix_pre_g[l]), w_main, w_qvt, w_f, b_f, row(sgu_norm_g[l]),
            w_spatial[l], b_spatial[l].T, bsz=bsz, seq=seq)
        ybt = _attention(qt, k, vt, negc)
        x2 = _out_ffn(x2, ya, ybt, sgb,
                      w_out[l].astype(BF16), row(mix_post_g[l]), row(ffn_pre_g[l]),
                      w_gate[l].astype(BF16), w_up[l].astype(BF16),
                      w_down[l].astype(BF16), row(ffn_post_g[l]), seq=seq)
    return x2.reshape(bsz, seq, d_model)
```

```python
import functools
import math

import jax
import jax.numpy as jnp
from jax import lax
from jax.experimental import pallas as pl
from jax.experimental.pallas import tpu as pltpu

F32 = jnp.float32
BF16 = jnp.bfloat16

EPS = 1e-6
LANES = 128
CHUNK = 128
N_GROUPS = 8
N_HEADS = 8
HEAD_DIM = 128
LOG2E = math.log2(math.e)
NEG_BIG = -1e30
VMEM_LIMIT = 56 * 1024 * 1024

N_SPLIT = 3
K_EXT = 128
QK_DIM = HEAD_DIM + K_EXT
V_ROWS = HEAD_DIM + 16

TOKEN_TILE = 256
ATTN_TILE = 512
BELOW_UNROLL = 4


def _gelu(x):
    c = math.sqrt(2.0 / math.pi)
    return 0.5 * x * (1.0 + jnp.tanh(c * (x + 0.044715 * (x * x * x))))


def _sigmoid(x):
    return 0.5 * (1.0 + jnp.tanh(0.5 * x))


def _log_sigmoid(x):
    return jnp.minimum(x, 0.0) - jnp.log1p(jnp.exp(-jnp.abs(x)))


def _rms_scale(x):
    return lax.rsqrt(jnp.mean(x * x, axis=-1, keepdims=True) + EPS)


def _resident(shape):
    nd = len(shape)
    return pl.BlockSpec(shape, lambda *_: (0,) * nd, pipeline_mode=pl.Buffered(1))


def _inproj_kernel(x_ref, gpre_ref, w_ref, wqv_ref, wf_ref, bf_ref, gv_ref, wsp_ref, bsp_ref,
                   ya_ref, qt_ref, k_ref, vt_ref, sgb_ref, carry_ref,
                   *, tiles_per_seq, d_model):
    tm = x_ref.shape[0]

    @pl.when(pl.program_id(0) % tiles_per_seq == 0)
    def _():
        carry_ref[...] = jnp.zeros_like(carry_ref)

    x = x_ref[...]
    h = (x * _rms_scale(x) * gpre_ref[...]).astype(BF16)

    def proj(seg):
        return jnp.dot(h, w_ref[:, seg * d_model:(seg + 1) * d_model],
                       preferred_element_type=F32)

    f = jnp.dot(h, wf_ref[...], preferred_element_type=F32) + bf_ref[...]
    logf = _log_sigmoid(f)
    hi = logf.astype(BF16)
    r1 = logf - hi.astype(F32)
    mid = r1.astype(BF16)
    lo = (r1 - mid.astype(F32)).astype(BF16)
    ti = lax.broadcasted_iota(jnp.int32, (tm, tm), 0)
    tj = lax.broadcasted_iota(jnp.int32, (tm, tm), 1)
    tri = jnp.where(tj <= ti, 1.0, 0.0).astype(BF16)

    gu = _gelu(proj(0))
    gv = _gelu(proj(1))

    csum = (jnp.dot(tri, hi, preferred_element_type=F32)
            + jnp.dot(tri, mid, preferred_element_type=F32)
            + jnp.dot(tri, lo, preferred_element_type=F32)) + carry_ref[...]
    carry_ref[...] = csum[tm - 1:tm, :]
    negc = csum * (-LOG2E)
    n_hi = negc.astype(BF16).astype(F32)
    n_r = negc - n_hi
    n_mid = n_r.astype(BF16).astype(F32)
    n_lo = n_r - n_mid
    lane = lax.broadcasted_iota(jnp.int32, (tm, K_EXT), 1)
    for hd in range(N_HEADS):
        bc = lambda v: jnp.broadcast_to(v[:, hd:hd + 1], (tm, K_EXT))
        ext = jnp.where(lane == 0, bc(n_hi),
                        jnp.where(lane == 1, bc(n_mid), jnp.where(lane == 2, bc(n_lo), 0.0)))
        k_ref[hd, :, HEAD_DIM:] = ext.astype(BF16)

    qvt = lax.dot_general(wqv_ref[...], h, (((1,), (1,)), ((), ())),
                          preferred_element_type=F32)
    qt = (qvt[:d_model] * (HEAD_DIM ** -0.5 * LOG2E)).astype(BF16)
    vt = qvt[d_model:].astype(BF16)
    kf = proj(2).astype(BF16)
    ext_row = lax.broadcasted_iota(jnp.int32, (K_EXT, tm), 0)
    q_ext = jnp.where(ext_row < N_SPLIT, 1.0, 0.0).astype(BF16)
    for hd in range(N_HEADS):
        hs = slice(hd * HEAD_DIM, (hd + 1) * HEAD_DIM)
        qt_ref[hd, :HEAD_DIM, :] = qt[hs]
        qt_ref[hd, HEAD_DIM:, :] = q_ext
        vt_ref[hd, :HEAD_DIM, :] = vt[hs]
        vt_ref[hd, HEAD_DIM:, :] = jnp.ones((V_ROWS - HEAD_DIM, tm), BF16)
        k_ref[hd, :, :HEAD_DIM] = kf[:, hs]
    sgb_ref[...] = _sigmoid(proj(4)).astype(BF16)
    sga = _sigmoid(proj(3))

    gc = gv - jnp.mean(gv, axis=-1, keepdims=True)
    vn = (gc * lax.rsqrt(jnp.mean(gc * gc, axis=-1, keepdims=True) + EPS)
          * gv_ref[...]).astype(BF16)
    row = lax.broadcasted_iota(jnp.int32, (CHUNK, CHUNK), 0)
    col = lax.broadcasted_iota(jnp.int32, (CHUNK, CHUNK), 1)
    causal = col <= row
    for g in range(N_GROUPS):
        wg = jnp.where(causal, wsp_ref[g], 0.0).astype(BF16)
        bcol = bsp_ref[:, g:g + 1]
        cs = slice(g * LANES, (g + 1) * LANES)
        for c in range(tm // CHUNK):
            rs = slice(c * CHUNK, (c + 1) * CHUNK)
            mixed = jnp.dot(wg, vn[rs, cs], preferred_element_type=F32) + bcol
            ya_ref[rs, cs] = (sga[rs, cs] * gu[rs, cs] * mixed).astype(BF16)


def _inproj(x2, gpre, w_main, w_qvt, w_f, b_f, g_v, w_sp, b_sp_t, *, bsz, seq):
    n_tok, d_model = x2.shape
    tm = TOKEN_TILE
    tps = seq // tm
    act = pl.BlockSpec((tm, d_model), lambda t: (t, 0))
    feat_major = lambda t: (t // tps, 0, 0, t % tps)
    tok_major = lambda t: (t // tps, 0, t % tps, 0)
    tok_bf = jax.ShapeDtypeStruct((n_tok, d_model), BF16)
    kern = functools.partial(_inproj_kernel, tiles_per_seq=tps, d_model=d_model)
    return pl.pallas_call(
        kern,
        grid=(n_tok // tm,),
        in_specs=[act, _resident(gpre.shape), _resident(w_main.shape), _resident(w_qvt.shape),
                  _resident(w_f.shape), _resident(b_f.shape), _resident(g_v.shape),
                  _resident(w_sp.shape), _resident(b_sp_t.shape)],
        out_specs=[act,
                   pl.BlockSpec((None, N_HEADS, QK_DIM, tm), feat_major),
                   pl.BlockSpec((None, N_HEADS, tm, QK_DIM), tok_major),
                   pl.BlockSpec((None, N_HEADS, V_ROWS, tm), feat_major),
                   act],
        out_shape=[tok_bf,
                   jax.ShapeDtypeStruct((bsz, N_HEADS, QK_DIM, seq), BF16),
                   jax.ShapeDtypeStruct((bsz, N_HEADS, seq, QK_DIM), BF16),
                   jax.ShapeDtypeStruct((bsz, N_HEADS, V_ROWS, seq), BF16),
                   tok_bf],
        scratch_shapes=[pltpu.VMEM((1, LANES), F32)],
        compiler_params=pltpu.CompilerParams(dimension_semantics=("arbitrary",),
                                             vmem_limit_bytes=VMEM_LIMIT),
        name="inproj_sgu",
    )(x2, gpre, w_main, w_qvt, w_f, b_f, g_v, w_sp, b_sp_t)


def _attn_kernel(qt_ref, k_ref, vt_ref, o_ref,
                 s0, s1, cm0, cm1, p0, p1, al0, al1, m_all, acc_all, *, n_tiles):
    t = ATTN_TILE
    bufs = ((s0, cm0, p0, al0), (s1, cm1, p1, al1))

    def tile(idx):
        return pl.ds(idx * t, t) if isinstance(idx, int) else pl.ds(pl.multiple_of(idx * t, t), t)

    def produce(pair, slot, masked):
        i, j = pair
        s_ref, cm_ref, _, _ = bufs[slot]
        s = jnp.dot(k_ref[tile(j), :], qt_ref[:, tile(i)], preferred_element_type=F32)
        if masked:
            key = lax.broadcasted_iota(jnp.int32, (t, t), 0)
            qry = lax.broadcasted_iota(jnp.int32, (t, t), 1)
            s = jnp.where(key <= qry, s, NEG_BIG)
        s_ref[...] = s
        cm_ref[...] = jnp.max(s, axis=0, keepdims=True)

    def softmax(pair, slot):
        i, _ = pair
        s_ref, cm_ref, p_ref, al_ref = bufs[slot]
        m_old = m_all[i]
        m_new = jnp.maximum(m_old, cm_ref[...])
        alpha = jnp.exp2(m_old - m_new)
        m_all[i] = m_new
        al_ref[...] = alpha
        p_ref[...] = jnp.exp2(s_ref[...] - m_new).astype(BF16)

    def accumulate(pair, slot):
        i, j = pair
        _, _, p_ref, al_ref = bufs[slot]
        pv = jnp.dot(vt_ref[:, tile(j)], p_ref[...], preferred_element_type=F32)
        acc_all[i] = al_ref[...] * acc_all[i] + pv

    def step(prev, cur, nxt, slot, nxt_masked):
        accumulate(prev, 1 - slot)
        produce(nxt, 1 - slot, nxt_masked)
        softmax(cur, slot)

    def below_next(pair):
        i, j = pair
        wrap = j + 1 == i
        return (jnp.where(wrap, jnp.minimum(i + 1, n_tiles - 1), i), jnp.where(wrap, 0, j + 1))

    m_all[...] = jnp.full_like(m_all, NEG_BIG)
    acc_all[...] = jnp.zeros_like(acc_all)
    al1[...] = jnp.ones_like(al1)
    p1[...] = jnp.zeros_like(p1)

    n_below = n_tiles * (n_tiles - 1) // 2
    assert n_tiles % 2 == 0 and n_below % BELOW_UNROLL == 0 and BELOW_UNROLL % 2 == 0
    d = n_tiles - 1
    produce((0, 0), 0, True)
    for u in range(n_tiles):
        prev = (max(u - 1, 0),) * 2
        if u < d:
            step(prev, (u, u), (u + 1, u + 1), u % 2, True)
        else:
            step(prev, (u, u), (1, 0), u % 2, False)

    def below_body(_, carry):
        prev, cur = carry
        for u in range(BELOW_UNROLL):
            nxt = below_next(cur)
            step(prev, cur, nxt, u % 2, False)
            prev, cur = cur, nxt
        return prev, cur

    i32 = lambda v: jnp.int32(v)
    last, _ = lax.fori_loop(0, n_below // BELOW_UNROLL, below_body,
                            ((i32(d), i32(d)), (i32(1), i32(0))))
    accumulate(last, 1)

    def finish(i, _):
        acc = acc_all[i]
        o_ref[:, tile(i)] = (acc[:HEAD_DIM] * (1.0 / acc[HEAD_DIM:HEAD_DIM + 1])).astype(BF16)
        return 0

    lax.fori_loop(0, n_tiles, finish, 0)


def _attention(qt, k, vt):
    bsz, n_heads, _, seq = qt.shape
    t = ATTN_TILE
    assert seq % t == 0
    n_tiles = seq // t
    per_head = lambda b, h: (b, h, 0, 0)
    kern = functools.partial(_attn_kernel, n_tiles=n_tiles)
    row = pltpu.VMEM((1, t), F32)
    return pl.pallas_call(
        kern,
        grid=(bsz, n_heads),
        in_specs=[pl.BlockSpec((None, None, QK_DIM, seq), per_head),
                  pl.BlockSpec((None, None, seq, QK_DIM), per_head),
                  pl.BlockSpec((None, None, V_ROWS, seq), per_head)],
        out_specs=pl.BlockSpec((None, HEAD_DIM, seq), lambda b, h: (b, h, 0)),
        out_shape=jax.ShapeDtypeStruct((bsz, n_heads * HEAD_DIM, seq), BF16),
        scratch_shapes=[pltpu.VMEM((t, t), F32), pltpu.VMEM((t, t), F32), row, row,
                        pltpu.VMEM((t, t), BF16), pltpu.VMEM((t, t), BF16), row, row,
                        pltpu.VMEM((n_tiles, 1, t), F32),
                        pltpu.VMEM((n_tiles, V_ROWS, t), F32)],
        compiler_params=pltpu.CompilerParams(
            dimension_semantics=("arbitrary", "arbitrary"),
            vmem_limit_bytes=VMEM_LIMIT),
        name="fox_attention",
    )(qt, k, vt)


def _out_ffn_kernel(x_ref, ya_ref, ybt_ref, sgb_ref, wo_ref, gpost_ref, gffn_ref,
                    wg_ref, wu_ref, wd_ref, gffn_post_ref, o_ref):
    yb = ybt_ref[...].astype(F32).T
    merged = (ya_ref[...].astype(F32) + sgb_ref[...].astype(F32) * yb).astype(BF16)
    mix = jnp.dot(merged, wo_ref[...], preferred_element_type=F32)
    x1 = x_ref[...] + mix * _rms_scale(mix) * gpost_ref[...]
    h = (x1 * _rms_scale(x1) * gffn_ref[...]).astype(BF16)
    gate = jnp.dot(h, wg_ref[...], preferred_element_type=F32)
    up = jnp.dot(h, wu_ref[...], preferred_element_type=F32)
    act = (gate * _sigmoid(gate) * up).astype(BF16)
    down = jnp.dot(act, wd_ref[...], preferred_element_type=F32)
    o_ref[...] = x1 + down * _rms_scale(down) * gffn_post_ref[...]


def _out_ffn(x2, ya, ybt, sgb, w_o, g_post, g_ffn, w_g, w_u, w_d, g_ffn_post, *, seq):
    n_tok, d_model = x2.shape
    tm = TOKEN_TILE
    tps = seq // tm
    act = pl.BlockSpec((tm, d_model), lambda i: (i, 0))
    feat = pl.BlockSpec((None, d_model, tm), lambda i: (i // tps, 0, i % tps))
    return pl.pallas_call(
        _out_ffn_kernel,
        grid=(n_tok // tm,),
        in_specs=[act, act, feat, act, _resident(w_o.shape), _resident(g_post.shape),
                  _resident(g_ffn.shape), _resident(w_g.shape), _resident(w_u.shape),
                  _resident(w_d.shape), _resident(g_ffn_post.shape)],
        out_specs=act,
        out_shape=jax.ShapeDtypeStruct((n_tok, d_model), F32),
        compiler_params=pltpu.CompilerParams(dimension_semantics=("arbitrary",),
                                             vmem_limit_bytes=VMEM_LIMIT),
        name="out_ffn",
    )(x2, ya, ybt, sgb, w_o, g_post, g_ffn, w_g, w_u, w_d, g_ffn_post)


def kernel(x, mix_pre_g, w_in, b_forget, sgu_norm_g, w_spatial, b_spatial, w_out, mix_post_g, ffn_pre_g, w_gate, w_up, w_down, ffn_post_g):
    bsz, seq, d_model = x.shape
    depth = w_in.shape[0]
    n_main = 7 * d_model
    assert seq % ATTN_TILE == 0 and seq % TOKEN_TILE == 0 and TOKEN_TILE % CHUNK == 0
    assert w_in.shape[2] == n_main + N_HEADS and d_model == N_HEADS * HEAD_DIM
    x2 = x.reshape(bsz * seq, d_model)
    row = lambda g: g.reshape(1, -1)
    seg = lambda w, s: w[:, s * d_model:(s + 1) * d_model]
    for l in range(depth):
        w = w_in[l]
        w_main = jnp.concatenate([seg(w, s) for s in (0, 1, 3, 5, 6)], axis=1).astype(BF16)
        w_qvt = jnp.concatenate([seg(w, 2).T, seg(w, 4).T], axis=0).astype(BF16)
        w_f = jnp.pad(w[:, n_main:], ((0, 0), (0, LANES - N_HEADS))).astype(BF16)
        b_f = jnp.pad(b_forget[l], (0, LANES - N_HEADS)).reshape(1, LANES)
        ya, qt, k, vt, sgb = _inproj(
            x2, row(mix_pre_g[l]), w_main, w_qvt, w_f, b_f, row(sgu_norm_g[l]),
            w_spatial[l], b_spatial[l].T, bsz=bsz, seq=seq)
        ybt = _attention(qt, k, vt)
        x2 = _out_ffn(x2, ya, ybt, sgb,
                      w_out[l].astype(BF16), row(mix_post_g[l]), row(ffn_pre_g[l]),
                      w_gate[l].astype(BF16), w_up[l].astype(BF16),
                      w_down[l].astype(BF16), row(ffn_post_g[l]), seq=seq)
    return x2.reshape(bsz, seq, d_model)
```

```python
import functools
import math

import jax
import jax.numpy as jnp
from jax import lax
from jax.experimental import pallas as pl
from jax.experimental.pallas import tpu as pltpu

F32 = jnp.float32
BF16 = jnp.bfloat16

EPS = 1e-6
LANES = 128
CHUNK = 128
N_GROUPS = 8
N_HEADS = 8
HEAD_DIM = 128
LOG2E = math.log2(math.e)
NEG_BIG = -1e30
VMEM_LIMIT = 56 * 1024 * 1024

N_SPLIT = 3
K_EXT = 128
QK_DIM = HEAD_DIM + K_EXT
V_ROWS = HEAD_DIM + 16

TOKEN_TILE = 512
PROJ_GROUPS = 2
FFN_TILE = 512
FFN_GROUPS = 2
ATTN_TILE = 512
BELOW_UNROLL = 4


def _gelu(x):
    c = math.sqrt(2.0 / math.pi)
    return 0.5 * x * (1.0 + jnp.tanh(c * (x + 0.044715 * (x * x * x))))


def _sigmoid(x):
    return 0.5 * (1.0 + jnp.tanh(0.5 * x))


def _log_sigmoid(x):
    return jnp.minimum(x, 0.0) - jnp.log1p(jnp.exp(-jnp.abs(x)))


def _rms_scale(x):
    return lax.rsqrt(jnp.mean(x * x, axis=-1, keepdims=True) + EPS)


def _resident(shape):
    nd = len(shape)
    return pl.BlockSpec(shape, lambda *_: (0,) * nd, pipeline_mode=pl.Buffered(1))


def _inproj_kernel(x_ref, gpre_ref, w_ref, wqv_ref, wf_ref, bf_ref, gv_ref, wsp_ref, bsp_ref,
                   ya_ref, qt_ref, k_ref, vt_ref, sgb_ref, carry_ref,
                   *, tiles_per_seq, d_model):
    tm = x_ref.shape[0]
    rows = tm // PROJ_GROUPS
    groups = [slice(r, r + rows) for r in range(0, tm, rows)]

    @pl.when(pl.program_id(0) % tiles_per_seq == 0)
    def _():
        carry_ref[...] = jnp.zeros_like(carry_ref)

    def proj(h, seg):
        return jnp.dot(h, w_ref[:, seg * d_model:(seg + 1) * d_model],
                       preferred_element_type=F32)

    hs, splits = [], []
    for rs in groups:
        x = x_ref[rs, :]
        h = (x * _rms_scale(x) * gpre_ref[...]).astype(BF16)
        hs.append(h)
        f = jnp.dot(h, wf_ref[...], preferred_element_type=F32) + bf_ref[...]
        logf = _log_sigmoid(f)
        hi = logf.astype(BF16)
        r1 = logf - hi.astype(F32)
        mid = r1.astype(BF16)
        splits.append((hi, mid, (r1 - mid.astype(F32)).astype(BF16)))
    ti = lax.broadcasted_iota(jnp.int32, (rows, rows), 0)
    tj = lax.broadcasted_iota(jnp.int32, (rows, rows), 1)
    tri = jnp.where(tj <= ti, 1.0, 0.0).astype(BF16)

    gus = [_gelu(proj(h, 0)) for h in hs]
    gvs = [_gelu(proj(h, 1)) for h in hs]

    lane = lax.broadcasted_iota(jnp.int32, (rows, K_EXT), 1)
    for rs, (hi, mid, lo) in zip(groups, splits):
        csum = (jnp.dot(tri, hi, preferred_element_type=F32)
                + jnp.dot(tri, mid, preferred_element_type=F32)
                + jnp.dot(tri, lo, preferred_element_type=F32)) + carry_ref[...]
        carry_ref[...] = csum[rows - 1:rows, :]
        negc = csum * (-LOG2E)
        n_hi = negc.astype(BF16).astype(F32)
        n_r = negc - n_hi
        n_mid = n_r.astype(BF16).astype(F32)
        n_lo = n_r - n_mid
        for hd in range(N_HEADS):
            bc = lambda v: jnp.broadcast_to(v[:, hd:hd + 1], (rows, K_EXT))
            ext = jnp.where(lane == 0, bc(n_hi),
                            jnp.where(lane == 1, bc(n_mid), jnp.where(lane == 2, bc(n_lo), 0.0)))
            k_ref[hd, rs, HEAD_DIM:] = ext.astype(BF16)

    ext_row = lax.broadcasted_iota(jnp.int32, (K_EXT, rows), 0)
    q_ext = jnp.where(ext_row < N_SPLIT, 1.0, 0.0).astype(BF16)
    sgas = []
    for rs, h in zip(groups, hs):
        qvt = lax.dot_general(wqv_ref[...], h, (((1,), (1,)), ((), ())),
                              preferred_element_type=F32)
        qt = (qvt[:d_model] * (HEAD_DIM ** -0.5 * LOG2E)).astype(BF16)
        vt = qvt[d_model:].astype(BF16)
        kf = proj(h, 2).astype(BF16)
        for hd in range(N_HEADS):
            cs = slice(hd * HEAD_DIM, (hd + 1) * HEAD_DIM)
            qt_ref[hd, :HEAD_DIM, rs] = qt[cs]
            qt_ref[hd, HEAD_DIM:, rs] = q_ext
            vt_ref[hd, :HEAD_DIM, rs] = vt[cs]
            vt_ref[hd, HEAD_DIM:, rs] = jnp.ones((V_ROWS - HEAD_DIM, rows), BF16)
            k_ref[hd, rs, :HEAD_DIM] = kf[:, cs]
        sgb_ref[rs, :] = _sigmoid(proj(h, 4)).astype(BF16)
        sgas.append(_sigmoid(proj(h, 3)))

    row = lax.broadcasted_iota(jnp.int32, (CHUNK, CHUNK), 0)
    col = lax.broadcasted_iota(jnp.int32, (CHUNK, CHUNK), 1)
    causal = col <= row
    vns = []
    for gv in gvs:
        gc = gv - jnp.mean(gv, axis=-1, keepdims=True)
        vns.append((gc * lax.rsqrt(jnp.mean(gc * gc, axis=-1, keepdims=True) + EPS)
                    * gv_ref[...]).astype(BF16))
    for g in range(N_GROUPS):
        wg = jnp.where(causal, wsp_ref[g], 0.0).astype(BF16)
        bcol = bsp_ref[:, g:g + 1]
        cs = slice(g * LANES, (g + 1) * LANES)
        for rs, vn, sga, gu in zip(groups, vns, sgas, gus):
            for c in range(rows // CHUNK):
                cr = slice(c * CHUNK, (c + 1) * CHUNK)
                mixed = jnp.dot(wg, vn[cr, cs], preferred_element_type=F32) + bcol
                ya_ref[rs.start + c * CHUNK:rs.start + (c + 1) * CHUNK, cs] = (
                    sga[cr, cs] * gu[cr, cs] * mixed).astype(BF16)


def _inproj(x2, gpre, w_main, w_qvt, w_f, b_f, g_v, w_sp, b_sp_t, *, bsz, seq):
    n_tok, d_model = x2.shape
    tm = TOKEN_TILE
    tps = seq // tm
    act = pl.BlockSpec((tm, d_model), lambda t: (t, 0))
    feat_major = lambda t: (t // tps, 0, 0, t % tps)
    tok_major = lambda t: (t // tps, 0, t % tps, 0)
    tok_bf = jax.ShapeDtypeStruct((n_tok, d_model), BF16)
    kern = functools.partial(_inproj_kernel, tiles_per_seq=tps, d_model=d_model)
    return pl.pallas_call(
        kern,
        grid=(n_tok // tm,),
        in_specs=[act, _resident(gpre.shape), _resident(w_main.shape), _resident(w_qvt.shape),
                  _resident(w_f.shape), _resident(b_f.shape), _resident(g_v.shape),
                  _resident(w_sp.shape), _resident(b_sp_t.shape)],
        out_specs=[act,
                   pl.BlockSpec((None, N_HEADS, QK_DIM, tm), feat_major),
                   pl.BlockSpec((None, N_HEADS, tm, QK_DIM), tok_major),
                   pl.BlockSpec((None, N_HEADS, V_ROWS, tm), feat_major),
                   act],
        out_shape=[tok_bf,
                   jax.ShapeDtypeStruct((bsz, N_HEADS, QK_DIM, seq), BF16),
                   jax.ShapeDtypeStruct((bsz, N_HEADS, seq, QK_DIM), BF16),
                   jax.ShapeDtypeStruct((bsz, N_HEADS, V_ROWS, seq), BF16),
                   tok_bf],
        scratch_shapes=[pltpu.VMEM((1, LANES), F32)],
        compiler_params=pltpu.CompilerParams(dimension_semantics=("arbitrary",),
                                             vmem_limit_bytes=VMEM_LIMIT),
        name="inproj_sgu",
    )(x2, gpre, w_main, w_qvt, w_f, b_f, g_v, w_sp, b_sp_t)


def _attn_kernel(qt_ref, k_ref, vt_ref, o_ref,
                 s0, s1, cm0, cm1, p0, p1, al0, al1, m_all, acc_all, *, n_tiles):
    t = ATTN_TILE
    bufs = ((s0, cm0, p0, al0), (s1, cm1, p1, al1))

    def tile(idx):
        return pl.ds(idx * t, t) if isinstance(idx, int) else pl.ds(pl.multiple_of(idx * t, t), t)

    def produce(pair, slot, masked):
        i, j = pair
        s_ref, cm_ref, _, _ = bufs[slot]
        s = jnp.dot(k_ref[tile(j), :], qt_ref[:, tile(i)], preferred_element_type=F32)
        if masked:
            key = lax.broadcasted_iota(jnp.int32, (t, t), 0)
            qry = lax.broadcasted_iota(jnp.int32, (t, t), 1)
            s = jnp.where(key <= qry, s, NEG_BIG)
        s_ref[...] = s
        cm_ref[...] = jnp.max(s, axis=0, keepdims=True)

    def softmax(pair, slot):
        i, _ = pair
        s_ref, cm_ref, p_ref, al_ref = bufs[slot]
        m_old = m_all[i]
        m_new = jnp.maximum(m_old, cm_ref[...])
        alpha = jnp.exp2(m_old - m_new)
        m_all[i] = m_new
        al_ref[...] = alpha
        p_ref[...] = jnp.exp2(s_ref[...] - m_new).astype(BF16)

    def accumulate(pair, slot):
        i, j = pair
        _, _, p_ref, al_ref = bufs[slot]
        pv = jnp.dot(vt_ref[:, tile(j)], p_ref[...], preferred_element_type=F32)
        acc_all[i] = al_ref[...] * acc_all[i] + pv

    def step(prev, cur, nxt, slot, nxt_masked):
        accumulate(prev, 1 - slot)
        produce(nxt, 1 - slot, nxt_masked)
        softmax(cur, slot)

    def below_next(pair):
        i, j = pair
        wrap = j + 1 == i
        return (jnp.where(wrap, jnp.minimum(i + 1, n_tiles - 1), i), jnp.where(wrap, 0, j + 1))

    m_all[...] = jnp.full_like(m_all, NEG_BIG)
    acc_all[...] = jnp.zeros_like(acc_all)
    al1[...] = jnp.ones_like(al1)
    p1[...] = jnp.zeros_like(p1)

    n_below = n_tiles * (n_tiles - 1) // 2
    assert n_tiles % 2 == 0 and n_below % BELOW_UNROLL == 0 and BELOW_UNROLL % 2 == 0
    d = n_tiles - 1
    produce((0, 0), 0, True)
    for u in range(n_tiles):
        prev = (max(u - 1, 0),) * 2
        if u < d:
            step(prev, (u, u), (u + 1, u + 1), u % 2, True)
        else:
            step(prev, (u, u), (1, 0), u % 2, False)

    def below_body(_, carry):
        prev, cur = carry
        for u in range(BELOW_UNROLL):
            nxt = below_next(cur)
            step(prev, cur, nxt, u % 2, False)
            prev, cur = cur, nxt
        return prev, cur

    i32 = lambda v: jnp.int32(v)
    last, _ = lax.fori_loop(0, n_below // BELOW_UNROLL, below_body,
                            ((i32(d), i32(d)), (i32(1), i32(0))))
    accumulate(last, 1)

    def finish(i, _):
        acc = acc_all[i]
        o_ref[:, tile(i)] = (acc[:HEAD_DIM] * (1.0 / acc[HEAD_DIM:HEAD_DIM + 1])).astype(BF16)
        return 0

    lax.fori_loop(0, n_tiles, finish, 0)


def _attention(qt, k, vt):
    bsz, n_heads, _, seq = qt.shape
    t = ATTN_TILE
    assert seq % t == 0
    n_tiles = seq // t
    per_head = lambda b, h: (b, h, 0, 0)
    kern = functools.partial(_attn_kernel, n_tiles=n_tiles)
    row = pltpu.VMEM((1, t), F32)
    return pl.pallas_call(
        kern,
        grid=(bsz, n_heads),
        in_specs=[pl.BlockSpec((None, None, QK_DIM, seq), per_head),
                  pl.BlockSpec((None, None, seq, QK_DIM), per_head),
                  pl.BlockSpec((None, None, V_ROWS, seq), per_head)],
        out_specs=pl.BlockSpec((None, HEAD_DIM, seq), lambda b, h: (b, h, 0)),
        out_shape=jax.ShapeDtypeStruct((bsz, n_heads * HEAD_DIM, seq), BF16),
        scratch_shapes=[pltpu.VMEM((t, t), F32), pltpu.VMEM((t, t), F32), row, row,
                        pltpu.VMEM((t, t), BF16), pltpu.VMEM((t, t), BF16), row, row,
                        pltpu.VMEM((n_tiles, 1, t), F32),
                        pltpu.VMEM((n_tiles, V_ROWS, t), F32)],
        compiler_params=pltpu.CompilerParams(
            dimension_semantics=("arbitrary", "arbitrary"),
            vmem_limit_bytes=VMEM_LIMIT),
        name="fox_attention",
    )(qt, k, vt)


def _out_ffn_kernel(x_ref, ya_ref, ybt_ref, sgb_ref, wo_ref, gpost_ref, gffn_ref,
                    wg_ref, wu_ref, wd_ref, gffn_post_ref, o_ref):
    tm = x_ref.shape[0]
    groups = [slice(r, r + tm // FFN_GROUPS) for r in range(0, tm, tm // FFN_GROUPS)]
    mix = []
    for rs in groups:
        yb = ybt_ref[:, rs].astype(F32).T
        merged = (ya_ref[rs, :].astype(F32) + sgb_ref[rs, :].astype(F32) * yb).astype(BF16)
        mix.append(jnp.dot(merged, wo_ref[...], preferred_element_type=F32))
    x1, gate, up = [], [], []
    for rs, mx in zip(groups, mix):
        x1.append(x_ref[rs, :] + mx * _rms_scale(mx) * gpost_ref[...])
        h = (x1[-1] * _rms_scale(x1[-1]) * gffn_ref[...]).astype(BF16)
        gate.append(jnp.dot(h, wg_ref[...], preferred_element_type=F32))
        up.append(jnp.dot(h, wu_ref[...], preferred_element_type=F32))
    down = []
    for g, u in zip(gate, up):
        act = (g * _sigmoid(g) * u).astype(BF16)
        down.append(jnp.dot(act, wd_ref[...], preferred_element_type=F32))
    for rs, xr, dn in zip(groups, x1, down):
        o_ref[rs, :] = xr + dn * _rms_scale(dn) * gffn_post_ref[...]


def _out_ffn(x2, ya, ybt, sgb, w_o, g_post, g_ffn, w_g, w_u, w_d, g_ffn_post, *, seq):
    n_tok, d_model = x2.shape
    tm = FFN_TILE
    tps = seq // tm
    act = pl.BlockSpec((tm, d_model), lambda i: (i, 0))
    feat = pl.BlockSpec((None, d_model, tm), lambda i: (i // tps, 0, i % tps))
    return pl.pallas_call(
        _out_ffn_kernel,
        grid=(n_tok // tm,),
        in_specs=[act, act, feat, act, _resident(w_o.shape), _resident(g_post.shape),
                  _resident(g_ffn.shape), _resident(w_g.shape), _resident(w_u.shape),
                  _resident(w_d.shape), _resident(g_ffn_post.shape)],
        out_specs=act,
        out_shape=jax.ShapeDtypeStruct((n_tok, d_model), F32),
        compiler_params=pltpu.CompilerParams(dimension_semantics=("arbitrary",),
                                             vmem_limit_bytes=VMEM_LIMIT),
        name="out_ffn",
    )(x2, ya, ybt, sgb, w_o, g_post, g_ffn, w_g, w_u, w_d, g_ffn_post)


def kernel(x, mix_pre_g, w_in, b_forget, sgu_norm_g, w_spatial, b_spatial, w_out, mix_post_g, ffn_pre_g, w_gate, w_up, w_down, ffn_post_g):
    bsz, seq, d_model = x.shape
    depth = w_in.shape[0]
    n_main = 7 * d_model
    assert seq % ATTN_TILE == 0 and seq % TOKEN_TILE == 0 and TOKEN_TILE % CHUNK == 0
    assert w_in.shape[2] == n_main + N_HEADS and d_model == N_HEADS * HEAD_DIM
    x2 = x.reshape(bsz * seq, d_model)
    row = lambda g: g.reshape(1, -1)
    seg = lambda w, s: w[:, s * d_model:(s + 1) * d_model]
    for l in range(depth):
        w = w_in[l]
        w_main = jnp.concatenate([seg(w, s) for s in (0, 1, 3, 5, 6)], axis=1).astype(BF16)
        w_qvt = jnp.concatenate([seg(w, 2).T, seg(w, 4).T], axis=0).astype(BF16)
        w_f = jnp.pad(w[:, n_main:], ((0, 0), (0, LANES - N_HEADS))).astype(BF16)
        b_f = jnp.pad(b_forget[l], (0, LANES - N_HEADS)).reshape(1, LANES)
        ya, qt, k, vt, sgb = _inproj(
            x2, row(mix_pre_g[l]), w_main, w_qvt, w_f, b_f, row(sgu_norm_g[l]),
            w_spatial[l], b_spatial[l].T, bsz=bsz, seq=seq)
        ybt = _attention(qt, k, vt)
        x2 = _out_ffn(x2, ya, ybt, sgb,
                      w_out[l].astype(BF16), row(mix_post_g[l]), row(ffn_pre_g[l]),
                      w_gate[l].astype(BF16), w_up[l].astype(BF16),
                      w_down[l].astype(BF16), row(ffn_post_g[l]), seq=seq)
    return x2.reshape(bsz, seq, d_model)
```

```python
import functools
import math

import jax
import jax.numpy as jnp
from jax import lax
from jax.experimental import pallas as pl
from jax.experimental.pallas import tpu as pltpu

F32 = jnp.float32
BF16 = jnp.bfloat16

EPS = 1e-6
LANES = 128
CHUNK = 128
N_GROUPS = 8
N_HEADS = 8
HEAD_DIM = 128
LOG2E = math.log2(math.e)
NEG_BIG = -1e30
VMEM_LIMIT = 56 * 1024 * 1024

N_SPLIT = 3
K_EXT = 128
QK_DIM = HEAD_DIM + K_EXT
V_ROWS = HEAD_DIM + 16

TOKEN_TILE = 512
PROJ_GROUPS = 2
FFN_TILE = 512
FFN_GROUPS = 2
ATTN_TILE = 512
LAG = 2
N_BUF = LAG + 1
LOOP_UNROLL = 6


def _gelu(x):
    c = math.sqrt(2.0 / math.pi)
    return 0.5 * x * (1.0 + jnp.tanh(c * (x + 0.044715 * (x * x * x))))


def _sigmoid(x):
    return 0.5 * (1.0 + jnp.tanh(0.5 * x))


def _log_sigmoid(x):
    return jnp.minimum(x, 0.0) - jnp.log1p(jnp.exp(-jnp.abs(x)))


def _rms_scale(x):
    return lax.rsqrt(jnp.mean(x * x, axis=-1, keepdims=True) + EPS)


def _resident(shape):
    nd = len(shape)
    return pl.BlockSpec(shape, lambda *_: (0,) * nd, pipeline_mode=pl.Buffered(1))


def _inproj_kernel(x_ref, gpre_ref, w_ref, wqv_ref, wf_ref, bf_ref, gv_ref, wsp_ref, bsp_ref,
                   ya_ref, qt_ref, k_ref, vt_ref, sgb_ref, carry_ref,
                   *, tiles_per_seq, d_model):
    tm = x_ref.shape[0]
    rows = tm // PROJ_GROUPS
    groups = [slice(r, r + rows) for r in range(0, tm, rows)]

    @pl.when(pl.program_id(0) % tiles_per_seq == 0)
    def _():
        carry_ref[...] = jnp.zeros_like(carry_ref)

    def proj(h, seg):
        return jnp.dot(h, w_ref[:, seg * d_model:(seg + 1) * d_model],
                       preferred_element_type=F32)

    hs, splits = [], []
    for rs in groups:
        x = x_ref[rs, :]
        h = (x * _rms_scale(x) * gpre_ref[...]).astype(BF16)
        hs.append(h)
        f = jnp.dot(h, wf_ref[...], preferred_element_type=F32) + bf_ref[...]
        logf = _log_sigmoid(f)
        hi = logf.astype(BF16)
        r1 = logf - hi.astype(F32)
        mid = r1.astype(BF16)
        splits.append((hi, mid, (r1 - mid.astype(F32)).astype(BF16)))
    ti = lax.broadcasted_iota(jnp.int32, (rows, rows), 0)
    tj = lax.broadcasted_iota(jnp.int32, (rows, rows), 1)
    tri = jnp.where(tj <= ti, 1.0, 0.0).astype(BF16)

    gus = [_gelu(proj(h, 0)) for h in hs]
    gvs = [_gelu(proj(h, 1)) for h in hs]

    lane = lax.broadcasted_iota(jnp.int32, (rows, K_EXT), 1)
    for rs, (hi, mid, lo) in zip(groups, splits):
        csum = (jnp.dot(tri, hi, preferred_element_type=F32)
                + jnp.dot(tri, mid, preferred_element_type=F32)
                + jnp.dot(tri, lo, preferred_element_type=F32)) + carry_ref[...]
        carry_ref[...] = csum[rows - 1:rows, :]
        negc = csum * (-LOG2E)
        n_hi = negc.astype(BF16).astype(F32)
        n_r = negc - n_hi
        n_mid = n_r.astype(BF16).astype(F32)
        n_lo = n_r - n_mid
        for hd in range(N_HEADS):
            bc = lambda v: jnp.broadcast_to(v[:, hd:hd + 1], (rows, K_EXT))
            ext = jnp.where(lane == 0, bc(n_hi),
                            jnp.where(lane == 1, bc(n_mid), jnp.where(lane == 2, bc(n_lo), 0.0)))
            k_ref[hd, rs, HEAD_DIM:] = ext.astype(BF16)

    ext_row = lax.broadcasted_iota(jnp.int32, (K_EXT, rows), 0)
    q_ext = jnp.where(ext_row < N_SPLIT, 1.0, 0.0).astype(BF16)
    sgas = []
    for rs, h in zip(groups, hs):
        qvt = lax.dot_general(wqv_ref[...], h, (((1,), (1,)), ((), ())),
                              preferred_element_type=F32)
        qt = (qvt[:d_model] * (HEAD_DIM ** -0.5 * LOG2E)).astype(BF16)
        vt = qvt[d_model:].astype(BF16)
        kf = proj(h, 2).astype(BF16)
        for hd in range(N_HEADS):
            cs = slice(hd * HEAD_DIM, (hd + 1) * HEAD_DIM)
            qt_ref[hd, :HEAD_DIM, rs] = qt[cs]
            qt_ref[hd, HEAD_DIM:, rs] = q_ext
            vt_ref[hd, :HEAD_DIM, rs] = vt[cs]
            vt_ref[hd, HEAD_DIM:, rs] = jnp.ones((V_ROWS - HEAD_DIM, rows), BF16)
            k_ref[hd, rs, :HEAD_DIM] = kf[:, cs]
        sgb_ref[rs, :] = _sigmoid(proj(h, 4)).astype(BF16)
        sgas.append(_sigmoid(proj(h, 3)))

    row = lax.broadcasted_iota(jnp.int32, (CHUNK, CHUNK), 0)
    col = lax.broadcasted_iota(jnp.int32, (CHUNK, CHUNK), 1)
    causal = col <= row
    vns = []
    for gv in gvs:
        gc = gv - jnp.mean(gv, axis=-1, keepdims=True)
        vns.append((gc * lax.rsqrt(jnp.mean(gc * gc, axis=-1, keepdims=True) + EPS)
                    * gv_ref[...]).astype(BF16))
    for g in range(N_GROUPS):
        wg = jnp.where(causal, wsp_ref[g], 0.0).astype(BF16)
        bcol = bsp_ref[:, g:g + 1]
        cs = slice(g * LANES, (g + 1) * LANES)
        for rs, vn, sga, gu in zip(groups, vns, sgas, gus):
            for c in range(rows // CHUNK):
                cr = slice(c * CHUNK, (c + 1) * CHUNK)
                mixed = jnp.dot(wg, vn[cr, cs], preferred_element_type=F32) + bcol
                ya_ref[rs.start + c * CHUNK:rs.start + (c + 1) * CHUNK, cs] = (
                    sga[cr, cs] * gu[cr, cs] * mixed).astype(BF16)


def _inproj(x2, gpre, w_main, w_qvt, w_f, b_f, g_v, w_sp, b_sp_t, *, bsz, seq):
    n_tok, d_model = x2.shape
    tm = TOKEN_TILE
    tps = seq // tm
    act = pl.BlockSpec((tm, d_model), lambda t: (t, 0))
    feat_major = lambda t: (t // tps, 0, 0, t % tps)
    tok_major = lambda t: (t // tps, 0, t % tps, 0)
    tok_bf = jax.ShapeDtypeStruct((n_tok, d_model), BF16)
    kern = functools.partial(_inproj_kernel, tiles_per_seq=tps, d_model=d_model)
    return pl.pallas_call(
        kern,
        grid=(n_tok // tm,),
        in_specs=[act, _resident(gpre.shape), _resident(w_main.shape), _resident(w_qvt.shape),
                  _resident(w_f.shape), _resident(b_f.shape), _resident(g_v.shape),
                  _resident(w_sp.shape), _resident(b_sp_t.shape)],
        out_specs=[act,
                   pl.BlockSpec((None, N_HEADS, QK_DIM, tm), feat_major),
                   pl.BlockSpec((None, N_HEADS, tm, QK_DIM), tok_major),
                   pl.BlockSpec((None, N_HEADS, V_ROWS, tm), feat_major),
                   act],
        out_shape=[tok_bf,
                   jax.ShapeDtypeStruct((bsz, N_HEADS, QK_DIM, seq), BF16),
                   jax.ShapeDtypeStruct((bsz, N_HEADS, seq, QK_DIM), BF16),
                   jax.ShapeDtypeStruct((bsz, N_HEADS, V_ROWS, seq), BF16),
                   tok_bf],
        scratch_shapes=[pltpu.VMEM((1, LANES), F32)],
        compiler_params=pltpu.CompilerParams(dimension_semantics=("arbitrary",),
                                             vmem_limit_bytes=VMEM_LIMIT),
        name="inproj_sgu",
    )(x2, gpre, w_main, w_qvt, w_f, b_f, g_v, w_sp, b_sp_t)


def _attn_kernel(qt_ref, k_ref, vt_ref, o_ref, *scratch, n_tiles):
    t = ATTN_TILE
    m_all, acc_all = scratch[-2:]
    bufs = tuple(scratch[4 * b:4 * b + 4] for b in range(N_BUF))

    def tile(idx):
        return pl.ds(idx * t, t) if isinstance(idx, int) else pl.ds(pl.multiple_of(idx * t, t), t)

    def produce(pair, slot, masked):
        i, j = pair
        s_ref, cm_ref, _, _ = bufs[slot]
        s = jnp.dot(k_ref[tile(j), :], qt_ref[:, tile(i)], preferred_element_type=F32)
        if masked:
            key = lax.broadcasted_iota(jnp.int32, (t, t), 0)
            qry = lax.broadcasted_iota(jnp.int32, (t, t), 1)
            s = jnp.where(key <= qry, s, NEG_BIG)
        s_ref[...] = s
        cm_ref[...] = jnp.max(s, axis=0, keepdims=True)

    def softmax(pair, slot):
        i, _ = pair
        s_ref, cm_ref, p_ref, al_ref = bufs[slot]
        m_old = m_all[i]
        m_new = jnp.maximum(m_old, cm_ref[...])
        alpha = jnp.exp2(m_old - m_new)
        m_all[i] = m_new
        al_ref[...] = alpha
        p_ref[...] = jnp.exp2(s_ref[...] - m_new).astype(BF16)

    def accumulate(pair, slot):
        i, j = pair
        _, _, p_ref, al_ref = bufs[slot]
        pv = jnp.dot(vt_ref[:, tile(j)], p_ref[...], preferred_element_type=F32)
        acc_all[i] = al_ref[...] * acc_all[i] + pv

    def step(n, behind, cur, ahead, ahead_masked):
        accumulate(behind, (n - LAG) % N_BUF)
        produce(ahead, (n + LAG) % N_BUF, ahead_masked)
        softmax(cur, n % N_BUF)

    def below_next(pair):
        i, j = pair
        wrap = j + 1 == i
        return (jnp.where(wrap, jnp.minimum(i + 1, n_tiles - 1), i), jnp.where(wrap, 0, j + 1))

    pairs = [(u, u) for u in range(n_tiles)] + [(i, j) for i in range(1, n_tiles) for j in range(i)]
    n_pairs = len(pairs)
    n_static = n_pairs - (n_pairs - n_tiles - LAG) // LOOP_UNROLL * LOOP_UNROLL
    assert LOOP_UNROLL % N_BUF == 0 and n_static >= n_tiles + LAG and n_static + LAG < n_pairs

    m_all[...] = jnp.full_like(m_all, NEG_BIG)
    acc_all[...] = jnp.zeros_like(acc_all)
    for n in range(LAG):
        _, _, p_ref, al_ref = bufs[(n - LAG) % N_BUF]
        al_ref[...] = jnp.ones_like(al_ref)
        p_ref[...] = jnp.zeros_like(p_ref)
        produce(pairs[n], n, True)

    for n in range(n_static):
        step(n, pairs[max(n - LAG, 0)], pairs[n], pairs[n + LAG], n + LAG < n_tiles)

    def loop_body(_, window):
        for u in range(LOOP_UNROLL):
            ahead = below_next(window[-1])
            step(n_static + u, window[0], window[LAG], ahead, False)
            window = window[1:] + (ahead,)
        return window

    first = tuple((jnp.int32(i), jnp.int32(j)) for i, j in pairs[n_static - LAG:n_static + LAG])
    window = lax.fori_loop(0, (n_pairs - n_static) // LOOP_UNROLL, loop_body, first)
    for n in range(LAG):
        accumulate(window[n], (n_pairs - LAG + n) % N_BUF)

    def finish(i, _):
        acc = acc_all[i]
        o_ref[:, tile(i)] = (acc[:HEAD_DIM] * (1.0 / acc[HEAD_DIM:HEAD_DIM + 1])).astype(BF16)
        return 0

    lax.fori_loop(0, n_tiles, finish, 0)


def _attention(qt, k, vt):
    bsz, n_heads, _, seq = qt.shape
    t = ATTN_TILE
    assert seq % t == 0
    n_tiles = seq // t
    per_head = lambda b, h: (b, h, 0, 0)
    kern = functools.partial(_attn_kernel, n_tiles=n_tiles)
    row = pltpu.VMEM((1, t), F32)
    return pl.pallas_call(
        kern,
        grid=(bsz, n_heads),
        in_specs=[pl.BlockSpec((None, None, QK_DIM, seq), per_head),
                  pl.BlockSpec((None, None, seq, QK_DIM), per_head),
                  pl.BlockSpec((None, None, V_ROWS, seq), per_head)],
        out_specs=pl.BlockSpec((None, HEAD_DIM, seq), lambda b, h: (b, h, 0)),
        out_shape=jax.ShapeDtypeStruct((bsz, n_heads * HEAD_DIM, seq), BF16),
        scratch_shapes=[pltpu.VMEM((t, t), F32), row, pltpu.VMEM((t, t), BF16), row] * N_BUF
        + [pltpu.VMEM((n_tiles, 1, t), F32), pltpu.VMEM((n_tiles, V_ROWS, t), F32)],
        compiler_params=pltpu.CompilerParams(
            dimension_semantics=("arbitrary", "arbitrary"),
            vmem_limit_bytes=VMEM_LIMIT),
        name="fox_attention",
    )(qt, k, vt)


def _out_ffn_kernel(x_ref, ya_ref, ybt_ref, sgb_ref, wo_ref, gpost_ref, gffn_ref,
                    wg_ref, wu_ref, wd_ref, gffn_post_ref, o_ref):
    tm = x_ref.shape[0]
    groups = [slice(r, r + tm // FFN_GROUPS) for r in range(0, tm, tm // FFN_GROUPS)]
    mix = []
    for rs in groups:
        yb = ybt_ref[:, rs].astype(F32).T
        merged = (ya_ref[rs, :].astype(F32) + sgb_ref[rs, :].astype(F32) * yb).astype(BF16)
        mix.append(jnp.dot(merged, wo_ref[...], preferred_element_type=F32))
    x1, gate, up = [], [], []
    for rs, mx in zip(groups, mix):
        x1.append(x_ref[rs, :] + mx * _rms_scale(mx) * gpost_ref[...])
        h = (x1[-1] * _rms_scale(x1[-1]) * gffn_ref[...]).astype(BF16)
        gate.append(jnp.dot(h, wg_ref[...], preferred_element_type=F32))
        up.append(jnp.dot(h, wu_ref[...], preferred_element_type=F32))
    down = []
    for g, u in zip(gate, up):
        act = (g * _sigmoid(g) * u).astype(BF16)
        down.append(jnp.dot(act, wd_ref[...], preferred_element_type=F32))
    for rs, xr, dn in zip(groups, x1, down):
        o_ref[rs, :] = xr + dn * _rms_scale(dn) * gffn_post_ref[...]


def _out_ffn(x2, ya, ybt, sgb, w_o, g_post, g_ffn, w_g, w_u, w_d, g_ffn_post, *, seq):
    n_tok, d_model = x2.shape
    tm = FFN_TILE
    tps = seq // tm
    act = pl.BlockSpec((tm, d_model), lambda i: (i, 0))
    feat = pl.BlockSpec((None, d_model, tm), lambda i: (i // tps, 0, i % tps))
    return pl.pallas_call(
        _out_ffn_kernel,
        grid=(n_tok // tm,),
        in_specs=[act, act, feat, act, _resident(w_o.shape), _resident(g_post.shape),
                  _resident(g_ffn.shape), _resident(w_g.shape), _resident(w_u.shape),
                  _resident(w_d.shape), _resident(g_ffn_post.shape)],
        out_specs=act,
        out_shape=jax.ShapeDtypeStruct((n_tok, d_model), F32),
        compiler_params=pltpu.CompilerParams(dimension_semantics=("arbitrary",),
                                             vmem_limit_bytes=VMEM_LIMIT),
        name="out_ffn",
    )(x2, ya, ybt, sgb, w_o, g_post, g_ffn, w_g, w_u, w_d, g_ffn_post)


def kernel(x, mix_pre_g, w_in, b_forget, sgu_norm_g, w_spatial, b_spatial, w_out, mix_post_g, ffn_pre_g, w_gate, w_up, w_down, ffn_post_g):
    bsz, seq, d_model = x.shape
    depth = w_in.shape[0]
    n_main = 7 * d_model
    assert seq % ATTN_TILE == 0 and seq % TOKEN_TILE == 0 and TOKEN_TILE % CHUNK == 0
    assert w_in.shape[2] == n_main + N_HEADS and d_model == N_HEADS * HEAD_DIM
    x2 = x.reshape(bsz * seq, d_model)
    row = lambda g: g.reshape(1, -1)
    seg = lambda w, s: w[:, s * d_model:(s + 1) * d_model]
    for l in range(depth):
        w = w_in[l]
        w_main = jnp.concatenate([seg(w, s) for s in (0, 1, 3, 5, 6)], axis=1).astype(BF16)
        w_qvt = jnp.concatenate([seg(w, 2).T, seg(w, 4).T], axis=0).astype(BF16)
        w_f = jnp.pad(w[:, n_main:], ((0, 0), (0, LANES - N_HEADS))).astype(BF16)
        b_f = jnp.pad(b_forget[l], (0, LANES - N_HEADS)).reshape(1, LANES)
        ya, qt, k, vt, sgb = _inproj(
            x2, row(mix_pre_g[l]), w_main, w_qvt, w_f, b_f, row(sgu_norm_g[l]),
            w_spatial[l], b_spatial[l].T, bsz=bsz, seq=seq)
        ybt = _attention(qt, k, vt)
        x2 = _out_ffn(x2, ya, ybt, sgb,
                      w_out[l].astype(BF16), row(mix_post_g[l]), row(ffn_pre_g[l]),
                      w_gate[l].astype(BF16), w_up[l].astype(BF16),
                      w_down[l].astype(BF16), row(ffn_post_g[l]), seq=seq)
    return x2.reshape(bsz, seq, d_model)
```

```python
import functools
import math

import jax
import jax.numpy as jnp
from jax import lax
from jax.experimental import pallas as pl
from jax.experimental.pallas import tpu as pltpu

F32 = jnp.float32
BF16 = jnp.bfloat16

EPS = 1e-6
LANES = 128
CHUNK = 128
N_GROUPS = 8
N_HEADS = 8
HEAD_DIM = 128
LOG2E = math.log2(math.e)
NEG_BIG = -1e30
VMEM_LIMIT = 56 * 1024 * 1024

N_SPLIT = 3
K_EXT = 128
QK_DIM = HEAD_DIM + K_EXT
V_ROWS = HEAD_DIM + 16

TOKEN_TILE = 512
PROJ_GROUPS = 2
FFN_TILE = 512
FFN_GROUPS = 2
ATTN_TILE = 512
LAG = 3
N_BUF = LAG + 1


def _gelu(x):
    c = math.sqrt(2.0 / math.pi)
    return 0.5 * x * (1.0 + jnp.tanh(c * (x + 0.044715 * (x * x * x))))


def _sigmoid(x):
    return 0.5 * (1.0 + jnp.tanh(0.5 * x))


def _log_sigmoid(x):
    return jnp.minimum(x, 0.0) - jnp.log1p(jnp.exp(-jnp.abs(x)))


def _rms_scale(x):
    return lax.rsqrt(jnp.mean(x * x, axis=-1, keepdims=True) + EPS)


def _resident(shape):
    nd = len(shape)
    return pl.BlockSpec(shape, lambda *_: (0,) * nd, pipeline_mode=pl.Buffered(1))


def _inproj_kernel(x_ref, gpre_ref, w_ref, wqv_ref, wf_ref, bf_ref, gv_ref, wsp_ref, bsp_ref,
                   ya_ref, qt_ref, k_ref, vt_ref, sgb_ref, carry_ref,
                   *, tiles_per_seq, d_model):
    tm = x_ref.shape[0]
    rows = tm // PROJ_GROUPS
    groups = [slice(r, r + rows) for r in range(0, tm, rows)]

    @pl.when(pl.program_id(0) % tiles_per_seq == 0)
    def _():
        carry_ref[...] = jnp.zeros_like(carry_ref)

    def proj(h, seg):
        return jnp.dot(h, w_ref[:, seg * d_model:(seg + 1) * d_model],
                       preferred_element_type=F32)

    hs, splits = [], []
    for rs in groups:
        x = x_ref[rs, :]
        h = (x * _rms_scale(x) * gpre_ref[...]).astype(BF16)
        hs.append(h)
        f = jnp.dot(h, wf_ref[...], preferred_element_type=F32) + bf_ref[...]
        logf = _log_sigmoid(f)
        hi = logf.astype(BF16)
        r1 = logf - hi.astype(F32)
        mid = r1.astype(BF16)
        splits.append((hi, mid, (r1 - mid.astype(F32)).astype(BF16)))
    ti = lax.broadcasted_iota(jnp.int32, (rows, rows), 0)
    tj = lax.broadcasted_iota(jnp.int32, (rows, rows), 1)
    tri = jnp.where(tj <= ti, 1.0, 0.0).astype(BF16)

    gus = [_gelu(proj(h, 0)) for h in hs]
    gvs = [_gelu(proj(h, 1)) for h in hs]

    lane = lax.broadcasted_iota(jnp.int32, (rows, K_EXT), 1)
    for rs, (hi, mid, lo) in zip(groups, splits):
        csum = (jnp.dot(tri, hi, preferred_element_type=F32)
                + jnp.dot(tri, mid, preferred_element_type=F32)
                + jnp.dot(tri, lo, preferred_element_type=F32)) + carry_ref[...]
        carry_ref[...] = csum[rows - 1:rows, :]
        negc = csum * (-LOG2E)
        n_hi = negc.astype(BF16).astype(F32)
        n_r = negc - n_hi
        n_mid = n_r.astype(BF16).astype(F32)
        n_lo = n_r - n_mid
        for hd in range(N_HEADS):
            bc = lambda v: jnp.broadcast_to(v[:, hd:hd + 1], (rows, K_EXT))
            ext = jnp.where(lane == 0, bc(n_hi),
                            jnp.where(lane == 1, bc(n_mid), jnp.where(lane == 2, bc(n_lo), 0.0)))
            k_ref[hd, rs, HEAD_DIM:] = ext.astype(BF16)

    ext_row = lax.broadcasted_iota(jnp.int32, (K_EXT, rows), 0)
    q_ext = jnp.where(ext_row < N_SPLIT, 1.0, 0.0).astype(BF16)
    sgas = []
    for rs, h in zip(groups, hs):
        qvt = lax.dot_general(wqv_ref[...], h, (((1,), (1,)), ((), ())),
                              preferred_element_type=F32)
        qt = (qvt[:d_model] * (HEAD_DIM ** -0.5 * LOG2E)).astype(BF16)
        vt = qvt[d_model:].astype(BF16)
        kf = proj(h, 2).astype(BF16)
        for hd in range(N_HEADS):
            cs = slice(hd * HEAD_DIM, (hd + 1) * HEAD_DIM)
            qt_ref[hd, :HEAD_DIM, rs] = qt[cs]
            qt_ref[hd, HEAD_DIM:, rs] = q_ext
            vt_ref[hd, :HEAD_DIM, rs] = vt[cs]
            vt_ref[hd, HEAD_DIM:, rs] = jnp.ones((V_ROWS - HEAD_DIM, rows), BF16)
            k_ref[hd, rs, :HEAD_DIM] = kf[:, cs]
        sgb_ref[rs, :] = _sigmoid(proj(h, 4)).astype(BF16)
        sgas.append(_sigmoid(proj(h, 3)))

    row = lax.broadcasted_iota(jnp.int32, (CHUNK, CHUNK), 0)
    col = lax.broadcasted_iota(jnp.int32, (CHUNK, CHUNK), 1)
    causal = col <= row
    vns = []
    for gv in gvs:
        gc = gv - jnp.mean(gv, axis=-1, keepdims=True)
        vns.append((gc * lax.rsqrt(jnp.mean(gc * gc, axis=-1, keepdims=True) + EPS)
                    * gv_ref[...]).astype(BF16))
    for g in range(N_GROUPS):
        wg = jnp.where(causal, wsp_ref[g], 0.0).astype(BF16)
        bcol = bsp_ref[:, g:g + 1]
        cs = slice(g * LANES, (g + 1) * LANES)
        for rs, vn, sga, gu in zip(groups, vns, sgas, gus):
            for c in range(rows // CHUNK):
                cr = slice(c * CHUNK, (c + 1) * CHUNK)
                mixed = jnp.dot(wg, vn[cr, cs], preferred_element_type=F32) + bcol
                ya_ref[rs.start + c * CHUNK:rs.start + (c + 1) * CHUNK, cs] = (
                    sga[cr, cs] * gu[cr, cs] * mixed).astype(BF16)


def _inproj(x2, gpre, w_main, w_qvt, w_f, b_f, g_v, w_sp, b_sp_t, *, bsz, seq):
    n_tok, d_model = x2.shape
    tm = TOKEN_TILE
    tps = seq // tm
    act = pl.BlockSpec((tm, d_model), lambda t: (t, 0))
    feat_major = lambda t: (t // tps, 0, 0, t % tps)
    tok_major = lambda t: (t // tps, 0, t % tps, 0)
    tok_bf = jax.ShapeDtypeStruct((n_tok, d_model), BF16)
    kern = functools.partial(_inproj_kernel, tiles_per_seq=tps, d_model=d_model)
    return pl.pallas_call(
        kern,
        grid=(n_tok // tm,),
        in_specs=[act, _resident(gpre.shape), _resident(w_main.shape), _resident(w_qvt.shape),
                  _resident(w_f.shape), _resident(b_f.shape), _resident(g_v.shape),
                  _resident(w_sp.shape), _resident(b_sp_t.shape)],
        out_specs=[act,
                   pl.BlockSpec((None, N_HEADS, QK_DIM, tm), feat_major),
                   pl.BlockSpec((None, N_HEADS, tm, QK_DIM), tok_major),
                   pl.BlockSpec((None, N_HEADS, V_ROWS, tm), feat_major),
                   act],
        out_shape=[tok_bf,
                   jax.ShapeDtypeStruct((bsz, N_HEADS, QK_DIM, seq), BF16),
                   jax.ShapeDtypeStruct((bsz, N_HEADS, seq, QK_DIM), BF16),
                   jax.ShapeDtypeStruct((bsz, N_HEADS, V_ROWS, seq), BF16),
                   tok_bf],
        scratch_shapes=[pltpu.VMEM((1, LANES), F32)],
        compiler_params=pltpu.CompilerParams(dimension_semantics=("arbitrary",),
                                             vmem_limit_bytes=VMEM_LIMIT),
        name="inproj_sgu",
    )(x2, gpre, w_main, w_qvt, w_f, b_f, g_v, w_sp, b_sp_t)


def _pair_order(n_tiles):
    return ([(u, u) for u in range(n_tiles)]
            + [(i, j) for i in range(1, n_tiles) for j in range(i)])


def _attn_kernel(qt_ref, k_ref, vt_ref, o_ref, *scratch, n_tiles):
    t = ATTN_TILE
    m_all, acc_all = scratch[-2:]
    bufs = tuple(scratch[4 * b:4 * b + 4] for b in range(N_BUF))

    def tile(idx):
        return pl.ds(idx * t, t) if isinstance(idx, int) else pl.ds(pl.multiple_of(idx * t, t), t)

    def produce(pair, slot, masked):
        i, j = pair
        s_ref, cm_ref, _, _ = bufs[slot]
        s = jnp.dot(k_ref[tile(j), :], qt_ref[:, tile(i)], preferred_element_type=F32)
        if masked:
            key = lax.broadcasted_iota(jnp.int32, (t, t), 0)
            qry = lax.broadcasted_iota(jnp.int32, (t, t), 1)
            s = jnp.where(key <= qry, s, NEG_BIG)
        s_ref[...] = s
        cm_ref[...] = jnp.max(s, axis=0, keepdims=True)

    def softmax(pair, slot):
        i, _ = pair
        s_ref, cm_ref, p_ref, al_ref = bufs[slot]
        m_old = m_all[i]
        m_new = jnp.maximum(m_old, cm_ref[...])
        alpha = jnp.exp2(m_old - m_new)
        m_all[i] = m_new
        al_ref[...] = alpha
        p_ref[...] = jnp.exp2(s_ref[...] - m_new).astype(BF16)

    def accumulate(pair, slot):
        i, j = pair
        _, _, p_ref, al_ref = bufs[slot]
        pv = jnp.dot(vt_ref[:, tile(j)], p_ref[...], preferred_element_type=F32)
        acc_all[i] = al_ref[...] * acc_all[i] + pv

    pairs = _pair_order(n_tiles)
    n_pairs = len(pairs)

    m_all[...] = jnp.full_like(m_all, NEG_BIG)
    acc_all[...] = jnp.zeros_like(acc_all)
    for n in range(LAG):
        _, _, p_ref, al_ref = bufs[(n - LAG) % N_BUF]
        al_ref[...] = jnp.ones_like(al_ref)
        p_ref[...] = jnp.zeros_like(p_ref)
        produce(pairs[n], n, True)

    for n in range(n_pairs):
        accumulate(pairs[max(n - LAG, 0)], (n - LAG) % N_BUF)
        if n + LAG < n_pairs:
            produce(pairs[n + LAG], (n + LAG) % N_BUF, n + LAG < n_tiles)
        softmax(pairs[n], n % N_BUF)
    for n in range(n_pairs - LAG, n_pairs):
        accumulate(pairs[n], n % N_BUF)

    for i in range(n_tiles):
        acc = acc_all[i]
        o_ref[:, tile(i)] = (acc[:HEAD_DIM] * (1.0 / acc[HEAD_DIM:HEAD_DIM + 1])).astype(BF16)


def _attention(qt, k, vt):
    bsz, n_heads, _, seq = qt.shape
    t = ATTN_TILE
    assert seq % t == 0
    n_tiles = seq // t
    per_head = lambda b, h: (b, h, 0, 0)
    kern = functools.partial(_attn_kernel, n_tiles=n_tiles)
    row = pltpu.VMEM((1, t), F32)
    return pl.pallas_call(
        kern,
        grid=(bsz, n_heads),
        in_specs=[pl.BlockSpec((None, None, QK_DIM, seq), per_head),
                  pl.BlockSpec((None, None, seq, QK_DIM), per_head),
                  pl.BlockSpec((None, None, V_ROWS, seq), per_head)],
        out_specs=pl.BlockSpec((None, HEAD_DIM, seq), lambda b, h: (b, h, 0)),
        out_shape=jax.ShapeDtypeStruct((bsz, n_heads * HEAD_DIM, seq), BF16),
        scratch_shapes=[pltpu.VMEM((t, t), F32), row, pltpu.VMEM((t, t), BF16), row] * N_BUF
        + [pltpu.VMEM((n_tiles, 1, t), F32), pltpu.VMEM((n_tiles, V_ROWS, t), F32)],
        compiler_params=pltpu.CompilerParams(
            dimension_semantics=("arbitrary", "arbitrary"),
            vmem_limit_bytes=VMEM_LIMIT),
        name="fox_attention",
    )(qt, k, vt)


def _out_ffn_kernel(x_ref, ya_ref, ybt_ref, sgb_ref, wo_ref, gpost_ref, gffn_ref,
                    wg_ref, wu_ref, wd_ref, gffn_post_ref, o_ref):
    tm = x_ref.shape[0]
    groups = [slice(r, r + tm // FFN_GROUPS) for r in range(0, tm, tm // FFN_GROUPS)]
    mix = []
    for rs in groups:
        yb = ybt_ref[:, rs].astype(F32).T
        merged = (ya_ref[rs, :].astype(F32) + sgb_ref[rs, :].astype(F32) * yb).astype(BF16)
        mix.append(jnp.dot(merged, wo_ref[...], preferred_element_type=F32))
    x1, gate, up = [], [], []
    for rs, mx in zip(groups, mix):
        x1.append(x_ref[rs, :] + mx * _rms_scale(mx) * gpost_ref[...])
        h = (x1[-1] * _rms_scale(x1[-1]) * gffn_ref[...]).astype(BF16)
        gate.append(jnp.dot(h, wg_ref[...], preferred_element_type=F32))
        up.append(jnp.dot(h, wu_ref[...], preferred_element_type=F32))
    down = []
    for g, u in zip(gate, up):
        act = (g * _sigmoid(g) * u).astype(BF16)
        down.append(jnp.dot(act, wd_ref[...], preferred_element_type=F32))
    for rs, xr, dn in zip(groups, x1, down):
        o_ref[rs, :] = xr + dn * _rms_scale(dn) * gffn_post_ref[...]


def _out_ffn(x2, ya, ybt, sgb, w_o, g_post, g_ffn, w_g, w_u, w_d, g_ffn_post, *, seq):
    n_tok, d_model = x2.shape
    tm = FFN_TILE
    tps = seq // tm
    act = pl.BlockSpec((tm, d_model), lambda i: (i, 0))
    feat = pl.BlockSpec((None, d_model, tm), lambda i: (i // tps, 0, i % tps))
    return pl.pallas_call(
        _out_ffn_kernel,
        grid=(n_tok // tm,),
        in_specs=[act, act, feat, act, _resident(w_o.shape), _resident(g_post.shape),
                  _resident(g_ffn.shape), _resident(w_g.shape), _resident(w_u.shape),
                  _resident(w_d.shape), _resident(g_ffn_post.shape)],
        out_specs=act,
        out_shape=jax.ShapeDtypeStruct((n_tok, d_model), F32),
        compiler_params=pltpu.CompilerParams(dimension_semantics=("arbitrary",),
                                             vmem_limit_bytes=VMEM_LIMIT),
        name="out_ffn",
    )(x2, ya, ybt, sgb, w_o, g_post, g_ffn, w_g, w_u, w_d, g_ffn_post)


def kernel(x, mix_pre_g, w_in, b_forget, sgu_norm_g, w_spatial, b_spatial, w_out, mix_post_g, ffn_pre_g, w_gate, w_up, w_down, ffn_post_g):
    bsz, seq, d_model = x.shape
    depth = w_in.shape[0]
    n_main = 7 * d_model
    assert seq % ATTN_TILE == 0 and seq % TOKEN_TILE == 0 and TOKEN_TILE % CHUNK == 0
    assert w_in.shape[2] == n_main + N_HEADS and d_model == N_HEADS * HEAD_DIM
    x2 = x.reshape(bsz * seq, d_model)
    row = lambda g: g.reshape(1, -1)
    seg = lambda w, s: w[:, s * d_model:(s + 1) * d_model]
    for l in range(depth):
        w = w_in[l]
        w_main = jnp.concatenate([seg(w, s) for s in (0, 1, 3, 5, 6)], axis=1).astype(BF16)
        w_qvt = jnp.concatenate([seg(w, 2).T, seg(w, 4).T], axis=0).astype(BF16)
        w_f = jnp.pad(w[:, n_main:], ((0, 0), (0, LANES - N_HEADS))).astype(BF16)
        b_f = jnp.pad(b_forget[l], (0, LANES - N_HEADS)).reshape(1, LANES)
        ya, qt, k, vt, sgb = _inproj(
            x2, row(mix_pre_g[l]), w_main, w_qvt, w_f, b_f, row(sgu_norm_g[l]),
            w_spatial[l], b_spatial[l].T, bsz=bsz, seq=seq)
        ybt = _attention(qt, k, vt)
        x2 = _out_ffn(x2, ya, ybt, sgb,
                      w_out[l].astype(BF16), row(mix_post_g[l]), row(ffn_pre_g[l]),
                      w_gate[l].astype(BF16), w_up[l].astype(BF16),
                      w_down[l].astype(BF16), row(ffn_post_g[l]), seq=seq)
    return x2.reshape(bsz, seq, d_model)
```

```python
import functools
import math

import jax
import jax.numpy as jnp
from jax import lax
from jax.experimental import pallas as pl
from jax.experimental.pallas import tpu as pltpu

F32 = jnp.float32
BF16 = jnp.bfloat16

EPS = 1e-6
LANES = 128
CHUNK = 128
N_GROUPS = 8
N_HEADS = 8
HEAD_DIM = 128
LOG2E = math.log2(math.e)
NEG_BIG = -1e30
VMEM_LIMIT = 56 * 1024 * 1024

N_SPLIT = 3
K_EXT = 128
QK_DIM = HEAD_DIM + K_EXT
V_ROWS = HEAD_DIM + 16

TOKEN_TILE = 512
PROJ_GROUPS = 2
FFN_TILE = 512
FFN_GROUPS = 2
ATTN_TILE = 512
LAG = 2


def _gelu(x):
    c = math.sqrt(2.0 / math.pi)
    return 0.5 * x * (1.0 + jnp.tanh(c * (x + 0.044715 * (x * x * x))))


def _sigmoid(x):
    return 0.5 * (1.0 + jnp.tanh(0.5 * x))


def _log_sigmoid(x):
    return jnp.minimum(x, 0.0) - jnp.log1p(jnp.exp(-jnp.abs(x)))


def _rms_scale(x):
    return lax.rsqrt(jnp.mean(x * x, axis=-1, keepdims=True) + EPS)


def _resident(shape):
    nd = len(shape)
    return pl.BlockSpec(shape, lambda *_: (0,) * nd, pipeline_mode=pl.Buffered(1))


def _inproj_kernel(x_ref, gpre_ref, w_ref, wqv_ref, wf_ref, bf_ref, gv_ref, wsp_ref, bsp_ref,
                   ya_ref, qt_ref, k_ref, vt_ref, sgb_ref, carry_ref,
                   *, tiles_per_seq, d_model):
    tm = x_ref.shape[0]
    rows = tm // PROJ_GROUPS
    groups = [slice(r, r + rows) for r in range(0, tm, rows)]

    @pl.when(pl.program_id(0) % tiles_per_seq == 0)
    def _():
        carry_ref[...] = jnp.zeros_like(carry_ref)

    def proj(h, seg):
        return jnp.dot(h, w_ref[:, seg * d_model:(seg + 1) * d_model],
                       preferred_element_type=F32)

    hs, splits = [], []
    for rs in groups:
        x = x_ref[rs, :]
        h = (x * _rms_scale(x) * gpre_ref[...]).astype(BF16)
        hs.append(h)
        f = jnp.dot(h, wf_ref[...], preferred_element_type=F32) + bf_ref[...]
        logf = _log_sigmoid(f)
        hi = logf.astype(BF16)
        r1 = logf - hi.astype(F32)
        mid = r1.astype(BF16)
        splits.append((hi, mid, (r1 - mid.astype(F32)).astype(BF16)))
    ti = lax.broadcasted_iota(jnp.int32, (rows, rows), 0)
    tj = lax.broadcasted_iota(jnp.int32, (rows, rows), 1)
    tri = jnp.where(tj <= ti, 1.0, 0.0).astype(BF16)

    gus = [_gelu(proj(h, 0)) for h in hs]
    gvs = [_gelu(proj(h, 1)) for h in hs]

    lane = lax.broadcasted_iota(jnp.int32, (rows, K_EXT), 1)
    for rs, (hi, mid, lo) in zip(groups, splits):
        csum = (jnp.dot(tri, hi, preferred_element_type=F32)
                + jnp.dot(tri, mid, preferred_element_type=F32)
                + jnp.dot(tri, lo, preferred_element_type=F32)) + carry_ref[...]
        carry_ref[...] = csum[rows - 1:rows, :]
        negc = csum * (-LOG2E)
        n_hi = negc.astype(BF16).astype(F32)
        n_r = negc - n_hi
        n_mid = n_r.astype(BF16).astype(F32)
        n_lo = n_r - n_mid
        for hd in range(N_HEADS):
            bc = lambda v: jnp.broadcast_to(v[:, hd:hd + 1], (rows, K_EXT))
            ext = jnp.where(lane == 0, bc(n_hi),
                            jnp.where(lane == 1, bc(n_mid), jnp.where(lane == 2, bc(n_lo), 0.0)))
            k_ref[hd, rs, HEAD_DIM:] = ext.astype(BF16)

    ext_row = lax.broadcasted_iota(jnp.int32, (K_EXT, rows), 0)
    q_ext = jnp.where(ext_row < N_SPLIT, 1.0, 0.0).astype(BF16)
    sgas = []
    for rs, h in zip(groups, hs):
        qvt = lax.dot_general(wqv_ref[...], h, (((1,), (1,)), ((), ())),
                              preferred_element_type=F32)
        qt = (qvt[:d_model] * (HEAD_DIM ** -0.5 * LOG2E)).astype(BF16)
        vt = qvt[d_model:].astype(BF16)
        kf = proj(h, 2).astype(BF16)
        for hd in range(N_HEADS):
            cs = slice(hd * HEAD_DIM, (hd + 1) * HEAD_DIM)
            qt_ref[hd, :HEAD_DIM, rs] = qt[cs]
            qt_ref[hd, HEAD_DIM:, rs] = q_ext
            vt_ref[hd, :HEAD_DIM, rs] = vt[cs]
            vt_ref[hd, HEAD_DIM:, rs] = jnp.ones((V_ROWS - HEAD_DIM, rows), BF16)
            k_ref[hd, rs, :HEAD_DIM] = kf[:, cs]
        sgb_ref[rs, :] = _sigmoid(proj(h, 4)).astype(BF16)
        sgas.append(_sigmoid(proj(h, 3)))

    row = lax.broadcasted_iota(jnp.int32, (CHUNK, CHUNK), 0)
    col = lax.broadcasted_iota(jnp.int32, (CHUNK, CHUNK), 1)
    causal = col <= row
    vns = []
    for gv in gvs:
        gc = gv - jnp.mean(gv, axis=-1, keepdims=True)
        vns.append((gc * lax.rsqrt(jnp.mean(gc * gc, axis=-1, keepdims=True) + EPS)
                    * gv_ref[...]).astype(BF16))
    for g in range(N_GROUPS):
        wg = jnp.where(causal, wsp_ref[g], 0.0).astype(BF16)
        bcol = bsp_ref[:, g:g + 1]
        cs = slice(g * LANES, (g + 1) * LANES)
        for rs, vn, sga, gu in zip(groups, vns, sgas, gus):
            for c in range(rows // CHUNK):
                cr = slice(c * CHUNK, (c + 1) * CHUNK)
                mixed = jnp.dot(wg, vn[cr, cs], preferred_element_type=F32) + bcol
                ya_ref[rs.start + c * CHUNK:rs.start + (c + 1) * CHUNK, cs] = (
                    sga[cr, cs] * gu[cr, cs] * mixed).astype(BF16)


def _inproj(x2, gpre, w_main, w_qvt, w_f, b_f, g_v, w_sp, b_sp_t, *, bsz, seq):
    n_tok, d_model = x2.shape
    tm = TOKEN_TILE
    tps = seq // tm
    act = pl.BlockSpec((tm, d_model), lambda t: (t, 0))
    feat_major = lambda t: (t // tps, 0, 0, t % tps)
    tok_major = lambda t: (t // tps, 0, t % tps, 0)
    tok_bf = jax.ShapeDtypeStruct((n_tok, d_model), BF16)
    kern = functools.partial(_inproj_kernel, tiles_per_seq=tps, d_model=d_model)
    return pl.pallas_call(
        kern,
        grid=(n_tok // tm,),
        in_specs=[act, _resident(gpre.shape), _resident(w_main.shape), _resident(w_qvt.shape),
                  _resident(w_f.shape), _resident(b_f.shape), _resident(g_v.shape),
                  _resident(w_sp.shape), _resident(b_sp_t.shape)],
        out_specs=[act,
                   pl.BlockSpec((None, N_HEADS, QK_DIM, tm), feat_major),
                   pl.BlockSpec((None, N_HEADS, tm, QK_DIM), tok_major),
                   pl.BlockSpec((None, N_HEADS, V_ROWS, tm), feat_major),
                   act],
        out_shape=[tok_bf,
                   jax.ShapeDtypeStruct((bsz, N_HEADS, QK_DIM, seq), BF16),
                   jax.ShapeDtypeStruct((bsz, N_HEADS, seq, QK_DIM), BF16),
                   jax.ShapeDtypeStruct((bsz, N_HEADS, V_ROWS, seq), BF16),
                   tok_bf],
        scratch_shapes=[pltpu.VMEM((1, LANES), F32)],
        compiler_params=pltpu.CompilerParams(dimension_semantics=("arbitrary",),
                                             vmem_limit_bytes=VMEM_LIMIT),
        name="inproj_sgu",
    )(x2, gpre, w_main, w_qvt, w_f, b_f, g_v, w_sp, b_sp_t)


def _pair_order(n_tiles):
    return ([(u, u) for u in range(n_tiles)]
            + [(i, j) for i in range(1, n_tiles) for j in range(i)])


def _attn_kernel(qt_ref, k_ref, vt_ref, o_ref, *scratch, n_tiles):
    t = ATTN_TILE
    m_all, acc_all = scratch

    def tile(idx):
        return pl.ds(idx * t, t)

    def scores(pair, masked):
        i, j = pair
        s = jnp.dot(k_ref[tile(j), :], qt_ref[:, tile(i)], preferred_element_type=F32)
        if masked:
            key = lax.broadcasted_iota(jnp.int32, (t, t), 0)
            qry = lax.broadcasted_iota(jnp.int32, (t, t), 1)
            s = jnp.where(key <= qry, s, NEG_BIG)
        return s, jnp.max(s, axis=0, keepdims=True)

    def softmax(pair, s, col_max):
        i, _ = pair
        m_old = m_all[i]
        m_new = jnp.maximum(m_old, col_max)
        m_all[i] = m_new
        return jnp.exp2(s - m_new).astype(BF16), jnp.exp2(m_old - m_new)

    def accumulate(pair, p, alpha):
        i, j = pair
        pv = jnp.dot(vt_ref[:, tile(j)], p, preferred_element_type=F32)
        acc_all[i] = alpha * acc_all[i] + pv

    pairs = _pair_order(n_tiles)
    n_pairs = len(pairs)

    m_all[...] = jnp.full_like(m_all, NEG_BIG)
    acc_all[...] = jnp.zeros_like(acc_all)

    scored = {n: scores(pairs[n], True) for n in range(LAG)}
    probs = {}
    for n in range(n_pairs):
        if n >= LAG:
            accumulate(pairs[n - LAG], *probs.pop(n - LAG))
        if n + LAG < n_pairs:
            scored[n + LAG] = scores(pairs[n + LAG], n + LAG < n_tiles)
        probs[n] = softmax(pairs[n], *scored.pop(n))
    for n in range(n_pairs - LAG, n_pairs):
        accumulate(pairs[n], *probs.pop(n))

    for i in range(n_tiles):
        acc = acc_all[i]
        o_ref[:, tile(i)] = (acc[:HEAD_DIM] * (1.0 / acc[HEAD_DIM:HEAD_DIM + 1])).astype(BF16)


def _attention(qt, k, vt):
    bsz, n_heads, _, seq = qt.shape
    t = ATTN_TILE
    assert seq % t == 0
    n_tiles = seq // t
    per_head = lambda b, h: (b, h, 0, 0)
    kern = functools.partial(_attn_kernel, n_tiles=n_tiles)
    row = pltpu.VMEM((1, t), F32)
    return pl.pallas_call(
        kern,
        grid=(bsz, n_heads),
        in_specs=[pl.BlockSpec((None, None, QK_DIM, seq), per_head),
                  pl.BlockSpec((None, None, seq, QK_DIM), per_head),
                  pl.BlockSpec((None, None, V_ROWS, seq), per_head)],
        out_specs=pl.BlockSpec((None, HEAD_DIM, seq), lambda b, h: (b, h, 0)),
        out_shape=jax.ShapeDtypeStruct((bsz, n_heads * HEAD_DIM, seq), BF16),
        scratch_shapes=[pltpu.VMEM((n_tiles, 1, t), F32),
                        pltpu.VMEM((n_tiles, V_ROWS, t), F32)],
        compiler_params=pltpu.CompilerParams(
            dimension_semantics=("arbitrary", "arbitrary"),
            vmem_limit_bytes=VMEM_LIMIT),
        name="fox_attention",
    )(qt, k, vt)


def _out_ffn_kernel(x_ref, ya_ref, ybt_ref, sgb_ref, wo_ref, gpost_ref, gffn_ref,
                    wg_ref, wu_ref, wd_ref, gffn_post_ref, o_ref):
    tm = x_ref.shape[0]
    groups = [slice(r, r + tm // FFN_GROUPS) for r in range(0, tm, tm // FFN_GROUPS)]
    mix = []
    for rs in groups:
        yb = ybt_ref[:, rs].astype(F32).T
        merged = (ya_ref[rs, :].astype(F32) + sgb_ref[rs, :].astype(F32) * yb).astype(BF16)
        mix.append(jnp.dot(merged, wo_ref[...], preferred_element_type=F32))
    x1, gate, up = [], [], []
    for rs, mx in zip(groups, mix):
        x1.append(x_ref[rs, :] + mx * _rms_scale(mx) * gpost_ref[...])
        h = (x1[-1] * _rms_scale(x1[-1]) * gffn_ref[...]).astype(BF16)
        gate.append(jnp.dot(h, wg_ref[...], preferred_element_type=F32))
        up.append(jnp.dot(h, wu_ref[...], preferred_element_type=F32))
    down = []
    for g, u in zip(gate, up):
        act = (g * _sigmoid(g) * u).astype(BF16)
        down.append(jnp.dot(act, wd_ref[...], preferred_element_type=F32))
    for rs, xr, dn in zip(groups, x1, down):
        o_ref[rs, :] = xr + dn * _rms_scale(dn) * gffn_post_ref[...]


def _out_ffn(x2, ya, ybt, sgb, w_o, g_post, g_ffn, w_g, w_u, w_d, g_ffn_post, *, seq):
    n_tok, d_model = x2.shape
    tm = FFN_TILE
    tps = seq // tm
    act = pl.BlockSpec((tm, d_model), lambda i: (i, 0))
    feat = pl.BlockSpec((None, d_model, tm), lambda i: (i // tps, 0, i % tps))
    return pl.pallas_call(
        _out_ffn_kernel,
        grid=(n_tok // tm,),
        in_specs=[act, act, feat, act, _resident(w_o.shape), _resident(g_post.shape),
                  _resident(g_ffn.shape), _resident(w_g.shape), _resident(w_u.shape),
                  _resident(w_d.shape), _resident(g_ffn_post.shape)],
        out_specs=act,
        out_shape=jax.ShapeDtypeStruct((n_tok, d_model), F32),
        compiler_params=pltpu.CompilerParams(dimension_semantics=("arbitrary",),
                                             vmem_limit_bytes=VMEM_LIMIT),
        name="out_ffn",
    )(x2, ya, ybt, sgb, w_o, g_post, g_ffn, w_g, w_u, w_d, g_ffn_post)


def kernel(x, mix_pre_g, w_in, b_forget, sgu_norm_g, w_spatial, b_spatial, w_out, mix_post_g, ffn_pre_g, w_gate, w_up, w_down, ffn_post_g):
    bsz, seq, d_model = x.shape
    depth = w_in.shape[0]
    n_main = 7 * d_model
    assert seq % ATTN_TILE == 0 and seq % TOKEN_TILE == 0 and TOKEN_TILE % CHUNK == 0
    assert w_in.shape[2] == n_main + N_HEADS and d_model == N_HEADS * HEAD_DIM
    x2 = x.reshape(bsz * seq, d_model)
    row = lambda g: g.reshape(1, -1)
    seg = lambda w, s: w[:, s * d_model:(s + 1) * d_model]
    for l in range(depth):
        w = w_in[l]
        w_main = jnp.concatenate([seg(w, s) for s in (0, 1, 3, 5, 6)], axis=1).astype(BF16)
        w_qvt = jnp.concatenate([seg(w, 2).T, seg(w, 4).T], axis=0).astype(BF16)
        w_f = jnp.pad(w[:, n_main:], ((0, 0), (0, LANES - N_HEADS))).astype(BF16)
        b_f = jnp.pad(b_forget[l], (0, LANES - N_HEADS)).reshape(1, LANES)
        ya, qt, k, vt, sgb = _inproj(
            x2, row(mix_pre_g[l]), w_main, w_qvt, w_f, b_f, row(sgu_norm_g[l]),
            w_spatial[l], b_spatial[l].T, bsz=bsz, seq=seq)
        ybt = _attention(qt, k, vt)
        x2 = _out_ffn(x2, ya, ybt, sgb,
                      w_out[l].astype(BF16), row(mix_post_g[l]), row(ffn_pre_g[l]),
                      w_gate[l].astype(BF16), w_up[l].astype(BF16),
                      w_down[l].astype(BF16), row(ffn_post_g[l]), seq=seq)
    return x2.reshape(bsz, seq, d_model)
```

```python
import functools
import math

import jax
import jax.numpy as jnp
from jax import lax
from jax.experimental import pallas as pl
from jax.experimental.pallas import tpu as pltpu

F32 = jnp.float32
BF16 = jnp.bfloat16

EPS = 1e-6
LANES = 128
CHUNK = 128
N_GROUPS = 8
N_HEADS = 8
HEAD_DIM = 128
LOG2E = math.log2(math.e)
NEG_BIG = -1e30
VMEM_LIMIT = 56 * 1024 * 1024

N_SPLIT = 3
K_EXT = 128
QK_DIM = HEAD_DIM + K_EXT
V_ROWS = HEAD_DIM + 16

TOKEN_TILE = 512
PROJ_GROUPS = 2
FFN_TILE = 512
FFN_GROUPS = 2
ATTN_TILE = 512
LAG = 2
DIAG_SPLIT = 2


def _gelu(x):
    c = math.sqrt(2.0 / math.pi)
    return 0.5 * x * (1.0 + jnp.tanh(c * (x + 0.044715 * (x * x * x))))


def _sigmoid(x):
    return 0.5 * (1.0 + jnp.tanh(0.5 * x))


def _log_sigmoid(x):
    return jnp.minimum(x, 0.0) - jnp.log1p(jnp.exp(-jnp.abs(x)))


def _rms_scale(x):
    return lax.rsqrt(jnp.mean(x * x, axis=-1, keepdims=True) + EPS)


def _resident(stacked, layer):
    tail = stacked.shape[1:]
    return pl.BlockSpec((None,) + tail, lambda *_: (layer,) + (0,) * len(tail),
                        pipeline_mode=pl.Buffered(1))


def _inproj_kernel(x_ref, gpre_ref, w_ref, wqv_ref, wf_ref, bf_ref, gv_ref, wsp_ref, bsp_ref,
                   ya_ref, qt_ref, k_ref, vt_ref, sgb_ref, carry_ref,
                   *, tiles_per_seq, d_model):
    tm = x_ref.shape[0]
    rows = tm // PROJ_GROUPS
    groups = [slice(r, r + rows) for r in range(0, tm, rows)]

    @pl.when(pl.program_id(0) % tiles_per_seq == 0)
    def _():
        carry_ref[...] = jnp.zeros_like(carry_ref)

    def proj(h, seg):
        return jnp.dot(h, w_ref[:, seg * d_model:(seg + 1) * d_model],
                       preferred_element_type=F32)

    hs, splits = [], []
    for rs in groups:
        x = x_ref[rs, :]
        h = (x * _rms_scale(x) * gpre_ref[...]).astype(BF16)
        hs.append(h)
        f = jnp.dot(h, wf_ref[...], preferred_element_type=F32) + bf_ref[...]
        logf = _log_sigmoid(f)
        hi = logf.astype(BF16)
        r1 = logf - hi.astype(F32)
        mid = r1.astype(BF16)
        splits.append((hi, mid, (r1 - mid.astype(F32)).astype(BF16)))
    ti = lax.broadcasted_iota(jnp.int32, (rows, rows), 0)
    tj = lax.broadcasted_iota(jnp.int32, (rows, rows), 1)
    tri = jnp.where(tj <= ti, 1.0, 0.0).astype(BF16)

    gus = [_gelu(proj(h, 0)) for h in hs]
    gvs = [_gelu(proj(h, 1)) for h in hs]

    lane = lax.broadcasted_iota(jnp.int32, (rows, K_EXT), 1)
    for rs, (hi, mid, lo) in zip(groups, splits):
        csum = (jnp.dot(tri, hi, preferred_element_type=F32)
                + jnp.dot(tri, mid, preferred_element_type=F32)
                + jnp.dot(tri, lo, preferred_element_type=F32)) + carry_ref[...]
        carry_ref[...] = csum[rows - 1:rows, :]
        negc = csum * (-LOG2E)
        n_hi = negc.astype(BF16).astype(F32)
        n_r = negc - n_hi
        n_mid = n_r.astype(BF16).astype(F32)
        n_lo = n_r - n_mid
        for hd in range(N_HEADS):
            bc = lambda v: jnp.broadcast_to(v[:, hd:hd + 1], (rows, K_EXT))
            ext = jnp.where(lane == 0, bc(n_hi),
                            jnp.where(lane == 1, bc(n_mid), jnp.where(lane == 2, bc(n_lo), 0.0)))
            k_ref[hd, rs, HEAD_DIM:] = ext.astype(BF16)

    ext_row = lax.broadcasted_iota(jnp.int32, (K_EXT, rows), 0)
    q_ext = jnp.where(ext_row < N_SPLIT, 1.0, 0.0).astype(BF16)
    sgas = []
    for rs, h in zip(groups, hs):
        qvt = lax.dot_general(wqv_ref[...], h, (((1,), (1,)), ((), ())),
                              preferred_element_type=F32)
        qt = (qvt[:d_model] * (HEAD_DIM ** -0.5 * LOG2E)).astype(BF16)
        vt = qvt[d_model:].astype(BF16)
        kf = proj(h, 2).astype(BF16)
        for hd in range(N_HEADS):
            cs = slice(hd * HEAD_DIM, (hd + 1) * HEAD_DIM)
            qt_ref[hd, :HEAD_DIM, rs] = qt[cs]
            qt_ref[hd, HEAD_DIM:, rs] = q_ext
            vt_ref[hd, :HEAD_DIM, rs] = vt[cs]
            vt_ref[hd, HEAD_DIM:, rs] = jnp.ones((V_ROWS - HEAD_DIM, rows), BF16)
            k_ref[hd, rs, :HEAD_DIM] = kf[:, cs]
        sgb_ref[rs, :] = _sigmoid(proj(h, 4)).astype(BF16)
        sgas.append(_sigmoid(proj(h, 3)))

    row = lax.broadcasted_iota(jnp.int32, (CHUNK, CHUNK), 0)
    col = lax.broadcasted_iota(jnp.int32, (CHUNK, CHUNK), 1)
    causal = col <= row
    vns = []
    for gv in gvs:
        gc = gv - jnp.mean(gv, axis=-1, keepdims=True)
        vns.append((gc * lax.rsqrt(jnp.mean(gc * gc, axis=-1, keepdims=True) + EPS)
                    * gv_ref[...]).astype(BF16))
    for g in range(N_GROUPS):
        wg = jnp.where(causal, wsp_ref[g], 0.0).astype(BF16)
        bcol = bsp_ref[:, g:g + 1]
        cs = slice(g * LANES, (g + 1) * LANES)
        for rs, vn, sga, gu in zip(groups, vns, sgas, gus):
            for c in range(rows // CHUNK):
                cr = slice(c * CHUNK, (c + 1) * CHUNK)
                mixed = jnp.dot(wg, vn[cr, cs], preferred_element_type=F32) + bcol
                ya_ref[rs.start + c * CHUNK:rs.start + (c + 1) * CHUNK, cs] = (
                    sga[cr, cs] * gu[cr, cs] * mixed).astype(BF16)


def _inproj(x2, params, *, layer, bsz, seq):
    n_tok, d_model = x2.shape
    tm = TOKEN_TILE
    tps = seq // tm
    act = pl.BlockSpec((tm, d_model), lambda t: (t, 0))
    feat_major = lambda t: (t // tps, 0, 0, t % tps)
    tok_major = lambda t: (t // tps, 0, t % tps, 0)
    tok_bf = jax.ShapeDtypeStruct((n_tok, d_model), BF16)
    kern = functools.partial(_inproj_kernel, tiles_per_seq=tps, d_model=d_model)
    return pl.pallas_call(
        kern,
        grid=(n_tok // tm,),
        in_specs=[act] + [_resident(p, layer) for p in params],
        out_specs=[act,
                   pl.BlockSpec((None, N_HEADS, QK_DIM, tm), feat_major),
                   pl.BlockSpec((None, N_HEADS, tm, QK_DIM), tok_major),
                   pl.BlockSpec((None, N_HEADS, V_ROWS, tm), feat_major),
                   act],
        out_shape=[tok_bf,
                   jax.ShapeDtypeStruct((bsz, N_HEADS, QK_DIM, seq), BF16),
                   jax.ShapeDtypeStruct((bsz, N_HEADS, seq, QK_DIM), BF16),
                   jax.ShapeDtypeStruct((bsz, N_HEADS, V_ROWS, seq), BF16),
                   tok_bf],
        scratch_shapes=[pltpu.VMEM((1, LANES), F32)],
        compiler_params=pltpu.CompilerParams(dimension_semantics=("arbitrary",),
                                             vmem_limit_bytes=VMEM_LIMIT),
        name="inproj_sgu",
    )(x2, *params)


def _pair_order(n_tiles):
    return ([(u, u) for u in range(n_tiles)]
            + [(i, j) for i in range(1, n_tiles) for j in range(i)])


def _attn_kernel(qt_ref, k_ref, vt_ref, o_ref, *scratch, n_tiles):
    t = ATTN_TILE
    m_all, acc_all = scratch

    def tile(idx):
        return pl.ds(idx * t, t)

    def blocks(masked):
        if not masked:
            return [(0, t, 0, t)]
        w = t // DIAG_SPLIT
        return [(0, (c + 1) * w, c * w, (c + 1) * w) for c in range(DIAG_SPLIT)]

    def scores(pair, masked):
        i, j = pair
        out = []
        for k0, k1, q0, q1 in blocks(masked):
            s = jnp.dot(k_ref[pl.ds(j * t + k0, k1 - k0), :], qt_ref[:, pl.ds(i * t + q0, q1 - q0)],
                        preferred_element_type=F32)
            if masked:
                key = lax.broadcasted_iota(jnp.int32, s.shape, 0) + k0
                qry = lax.broadcasted_iota(jnp.int32, s.shape, 1) + q0
                s = jnp.where(key <= qry, s, NEG_BIG)
            out.append((s, jnp.max(s, axis=0, keepdims=True)))
        return masked, out

    def softmax(pair, masked, scored_blocks):
        i, _ = pair
        out = []
        for (k0, k1, q0, q1), (s, col_max) in zip(blocks(masked), scored_blocks):
            m_old = m_all[i, :, q0:q1]
            m_new = jnp.maximum(m_old, col_max)
            m_all[i, :, q0:q1] = m_new
            out.append((jnp.exp2(s - m_new).astype(BF16), jnp.exp2(m_old - m_new)))
        return masked, out

    def accumulate(pair, masked, prob_blocks):
        i, j = pair
        for (k0, k1, q0, q1), (p, alpha) in zip(blocks(masked), prob_blocks):
            pv = jnp.dot(vt_ref[:, pl.ds(j * t + k0, k1 - k0)], p, preferred_element_type=F32)
            acc_all[i, :, q0:q1] = alpha * acc_all[i, :, q0:q1] + pv

    pairs = _pair_order(n_tiles)
    n_pairs = len(pairs)

    m_all[...] = jnp.full_like(m_all, NEG_BIG)
    acc_all[...] = jnp.zeros_like(acc_all)

    scored = {n: scores(pairs[n], True) for n in range(LAG)}
    probs = {}
    for n in range(n_pairs):
        if n >= LAG:
            accumulate(pairs[n - LAG], *probs.pop(n - LAG))
        if n + LAG < n_pairs:
            scored[n + LAG] = scores(pairs[n + LAG], n + LAG < n_tiles)
        probs[n] = softmax(pairs[n], *scored.pop(n))
    for n in range(n_pairs - LAG, n_pairs):
        accumulate(pairs[n], *probs.pop(n))

    for i in range(n_tiles):
        acc = acc_all[i]
        o_ref[:, tile(i)] = (acc[:HEAD_DIM] * (1.0 / acc[HEAD_DIM:HEAD_DIM + 1])).astype(BF16)


def _attention(qt, k, vt):
    bsz, n_heads, _, seq = qt.shape
    t = ATTN_TILE
    assert seq % t == 0
    n_tiles = seq // t
    per_head = lambda b, h: (b, h, 0, 0)
    kern = functools.partial(_attn_kernel, n_tiles=n_tiles)
    row = pltpu.VMEM((1, t), F32)
    return pl.pallas_call(
        kern,
        grid=(bsz, n_heads),
        in_specs=[pl.BlockSpec((None, None, QK_DIM, seq), per_head),
                  pl.BlockSpec((None, None, seq, QK_DIM), per_head),
                  pl.BlockSpec((None, None, V_ROWS, seq), per_head)],
        out_specs=pl.BlockSpec((None, HEAD_DIM, seq), lambda b, h: (b, h, 0)),
        out_shape=jax.ShapeDtypeStruct((bsz, n_heads * HEAD_DIM, seq), BF16),
        scratch_shapes=[pltpu.VMEM((n_tiles, 1, t), F32),
                        pltpu.VMEM((n_tiles, V_ROWS, t), F32)],
        compiler_params=pltpu.CompilerParams(
            dimension_semantics=("arbitrary", "arbitrary"),
            vmem_limit_bytes=VMEM_LIMIT),
        name="fox_attention",
    )(qt, k, vt)


def _out_ffn_kernel(x_ref, ya_ref, ybt_ref, sgb_ref, wo_ref, gpost_ref, gffn_ref,
                    wg_ref, wu_ref, wd_ref, gffn_post_ref, o_ref):
    tm = x_ref.shape[0]
    groups = [slice(r, r + tm // FFN_GROUPS) for r in range(0, tm, tm // FFN_GROUPS)]
    mix = []
    for rs in groups:
        yb = ybt_ref[:, rs].astype(F32).T
        merged = (ya_ref[rs, :].astype(F32) + sgb_ref[rs, :].astype(F32) * yb).astype(BF16)
        mix.append(jnp.dot(merged, wo_ref[...], preferred_element_type=F32))
    x1, gate, up = [], [], []
    for rs, mx in zip(groups, mix):
        x1.append(x_ref[rs, :] + mx * _rms_scale(mx) * gpost_ref[...])
        h = (x1[-1] * _rms_scale(x1[-1]) * gffn_ref[...]).astype(BF16)
        gate.append(jnp.dot(h, wg_ref[...], preferred_element_type=F32))
        up.append(jnp.dot(h, wu_ref[...], preferred_element_type=F32))
    down = []
    for g, u in zip(gate, up):
        act = (g * _sigmoid(g) * u).astype(BF16)
        down.append(jnp.dot(act, wd_ref[...], preferred_element_type=F32))
    for rs, xr, dn in zip(groups, x1, down):
        o_ref[rs, :] = xr + dn * _rms_scale(dn) * gffn_post_ref[...]


def _out_ffn(x2, ya, ybt, sgb, params, *, layer, seq):
    n_tok, d_model = x2.shape
    tm = FFN_TILE
    tps = seq // tm
    act = pl.BlockSpec((tm, d_model), lambda i: (i, 0))
    feat = pl.BlockSpec((None, d_model, tm), lambda i: (i // tps, 0, i % tps))
    return pl.pallas_call(
        _out_ffn_kernel,
        grid=(n_tok // tm,),
        in_specs=[act, act, feat, act] + [_resident(p, layer) for p in params],
        out_specs=act,
        out_shape=jax.ShapeDtypeStruct((n_tok, d_model), F32),
        compiler_params=pltpu.CompilerParams(dimension_semantics=("arbitrary",),
                                             vmem_limit_bytes=VMEM_LIMIT),
        name="out_ffn",
    )(x2, ya, ybt, sgb, *params)


def kernel(x, mix_pre_g, w_in, b_forget, sgu_norm_g, w_spatial, b_spatial, w_out, mix_post_g, ffn_pre_g, w_gate, w_up, w_down, ffn_post_g):
    bsz, seq, d_model = x.shape
    depth = w_in.shape[0]
    n_main = 7 * d_model
    assert seq % ATTN_TILE == 0 and seq % TOKEN_TILE == 0 and TOKEN_TILE % CHUNK == 0
    assert w_in.shape[2] == n_main + N_HEADS and d_model == N_HEADS * HEAD_DIM
    x2 = x.reshape(bsz * seq, d_model)
    rows = lambda g: g.reshape(depth, 1, -1)
    seg = lambda s: w_in[:, :, s * d_model:(s + 1) * d_model]
    w_main = jnp.concatenate([seg(s) for s in (0, 1, 3, 5, 6)], axis=2).astype(BF16)
    w_qvt = jnp.concatenate([seg(2), seg(4)], axis=2).transpose(0, 2, 1).astype(BF16)
    w_f = jnp.pad(w_in[:, :, n_main:], ((0, 0), (0, 0), (0, LANES - N_HEADS))).astype(BF16)
    b_f = rows(jnp.pad(b_forget, ((0, 0), (0, LANES - N_HEADS))))
    proj_params = (rows(mix_pre_g), w_main, w_qvt, w_f, b_f, rows(sgu_norm_g),
                   w_spatial, b_spatial.transpose(0, 2, 1))
    ffn_params = (w_out.astype(BF16), rows(mix_post_g), rows(ffn_pre_g), w_gate.astype(BF16),
                  w_up.astype(BF16), w_down.astype(BF16), rows(ffn_post_g))
    for l in range(depth):
        ya, qt, k, vt, sgb = _inproj(x2, proj_params, layer=l, bsz=bsz, seq=seq)
        ybt = _attention(qt, k, vt)
        x2 = _out_ffn(x2, ya, ybt, sgb, ffn_params, layer=l, seq=seq)
    return x2.reshape(bsz, seq, d_model)
```

```python
import functools
import math

import jax
import jax.numpy as jnp
from jax import lax
from jax.experimental import pallas as pl
from jax.experimental.pallas import tpu as pltpu

F32 = jnp.float32
BF16 = jnp.bfloat16

EPS = 1e-6
LANES = 128
CHUNK = 128
N_GROUPS = 8
N_HEADS = 8
HEAD_DIM = 128
LOG2E = math.log2(math.e)
NEG_BIG = -1e30
VMEM_LIMIT = 56 * 1024 * 1024

N_SPLIT = 3
K_EXT = 128
QK_DIM = HEAD_DIM + K_EXT
V_ROWS = HEAD_DIM + 16

TOKEN_TILE = 512
PROJ_GROUPS = 2
FFN_TILE = 512
FFN_GROUPS = 2
ATTN_TILE = 512
LAG = 2
DIAG_SPLIT = 2


def _gelu(x):
    c = math.sqrt(2.0 / math.pi)
    return 0.5 * x * (1.0 + jnp.tanh(c * (x + 0.044715 * (x * x * x))))


def _sigmoid(x):
    return 0.5 * (1.0 + jnp.tanh(0.5 * x))


def _log_sigmoid(x):
    return jnp.minimum(x, 0.0) - jnp.log1p(jnp.exp(-jnp.abs(x)))


def _rms_scale(x):
    return lax.rsqrt(jnp.mean(x * x, axis=-1, keepdims=True) + EPS)


def _resident(stacked, layer):
    tail = stacked.shape[1:]
    return pl.BlockSpec((None,) + tail, lambda *_: (layer,) + (0,) * len(tail),
                        pipeline_mode=pl.Buffered(1))


def _inproj_kernel(x_ref, gpre_ref, w_ref, wqv_ref, wf_ref, bf_ref, gv_ref, wsp_ref, bsp_ref,
                   ya_ref, qt_ref, k_ref, vt_ref, sgb_ref, carry_ref,
                   *, tiles_per_seq, d_model):
    tm = x_ref.shape[0]
    rows = tm // PROJ_GROUPS
    groups = [slice(r, r + rows) for r in range(0, tm, rows)]

    @pl.when(pl.program_id(0) % tiles_per_seq == 0)
    def _():
        carry_ref[...] = jnp.zeros_like(carry_ref)

    def proj(h, seg):
        return jnp.dot(h, w_ref[:, seg * d_model:(seg + 1) * d_model],
                       preferred_element_type=F32)

    hs, splits = [], []
    for rs in groups:
        x = x_ref[rs, :]
        h = (x * _rms_scale(x) * gpre_ref[...]).astype(BF16)
        hs.append(h)
        f = jnp.dot(h, wf_ref[...], preferred_element_type=F32) + bf_ref[...]
        logf = _log_sigmoid(f)
        hi = logf.astype(BF16)
        r1 = logf - hi.astype(F32)
        mid = r1.astype(BF16)
        splits.append((hi, mid, (r1 - mid.astype(F32)).astype(BF16)))
    ti = lax.broadcasted_iota(jnp.int32, (rows, rows), 0)
    tj = lax.broadcasted_iota(jnp.int32, (rows, rows), 1)
    tri = jnp.where(tj <= ti, 1.0, 0.0).astype(BF16)

    gus = [_gelu(proj(h, 0)) for h in hs]
    gvs = [_gelu(proj(h, 1)) for h in hs]

    lane = lax.broadcasted_iota(jnp.int32, (rows, K_EXT), 1)
    for rs, (hi, mid, lo) in zip(groups, splits):
        csum = (jnp.dot(tri, hi, preferred_element_type=F32)
                + jnp.dot(tri, mid, preferred_element_type=F32)
                + jnp.dot(tri, lo, preferred_element_type=F32)) + carry_ref[...]
        carry_ref[...] = csum[rows - 1:rows, :]
        negc = csum * (-LOG2E)
        n_hi = negc.astype(BF16).astype(F32)
        n_r = negc - n_hi
        n_mid = n_r.astype(BF16).astype(F32)
        n_lo = n_r - n_mid
        for hd in range(N_HEADS):
            bc = lambda v: jnp.broadcast_to(v[:, hd:hd + 1], (rows, K_EXT))
            ext = jnp.where(lane == 0, bc(n_hi),
                            jnp.where(lane == 1, bc(n_mid), jnp.where(lane == 2, bc(n_lo), 0.0)))
            k_ref[hd, rs, HEAD_DIM:] = ext.astype(BF16)

    ext_row = lax.broadcasted_iota(jnp.int32, (K_EXT, rows), 0)
    q_ext = jnp.where(ext_row < N_SPLIT, 1.0, 0.0).astype(BF16)
    sgas = []
    for rs, h in zip(groups, hs):
        qvt = lax.dot_general(wqv_ref[...], h, (((1,), (1,)), ((), ())),
                              preferred_element_type=F32)
        qt = (qvt[:d_model] * (HEAD_DIM ** -0.5 * LOG2E)).astype(BF16)
        vt = qvt[d_model:].astype(BF16)
        kf = proj(h, 2).astype(BF16)
        for hd in range(N_HEADS):
            cs = slice(hd * HEAD_DIM, (hd + 1) * HEAD_DIM)
            qt_ref[hd, :HEAD_DIM, rs] = qt[cs]
            qt_ref[hd, HEAD_DIM:, rs] = q_ext
            vt_ref[hd, :HEAD_DIM, rs] = vt[cs]
            vt_ref[hd, HEAD_DIM:, rs] = jnp.ones((V_ROWS - HEAD_DIM, rows), BF16)
            k_ref[hd, rs, :HEAD_DIM] = kf[:, cs]
        sgb_ref[rs, :] = _sigmoid(proj(h, 4)).astype(BF16)
        sgas.append(_sigmoid(proj(h, 3)))

    row = lax.broadcasted_iota(jnp.int32, (CHUNK, CHUNK), 0)
    col = lax.broadcasted_iota(jnp.int32, (CHUNK, CHUNK), 1)
    causal = col <= row
    vns = []
    for gv in gvs:
        gc = gv - jnp.mean(gv, axis=-1, keepdims=True)
        vns.append((gc * lax.rsqrt(jnp.mean(gc * gc, axis=-1, keepdims=True) + EPS)
                    * gv_ref[...]).astype(BF16))
    for g in range(N_GROUPS):
        wg = jnp.where(causal, wsp_ref[g], 0.0).astype(BF16)
        bcol = bsp_ref[:, g:g + 1]
        cs = slice(g * LANES, (g + 1) * LANES)
        for rs, vn, sga, gu in zip(groups, vns, sgas, gus):
            for c in range(rows // CHUNK):
                cr = slice(c * CHUNK, (c + 1) * CHUNK)
                mixed = jnp.dot(wg, vn[cr, cs], preferred_element_type=F32) + bcol
                ya_ref[rs.start + c * CHUNK:rs.start + (c + 1) * CHUNK, cs] = (
                    sga[cr, cs] * gu[cr, cs] * mixed).astype(BF16)


def _inproj(x2, params, *, layer, bsz, seq):
    n_tok, d_model = x2.shape
    tm = TOKEN_TILE
    tps = seq // tm
    act = pl.BlockSpec((tm, d_model), lambda t: (t, 0))
    feat_major = lambda t: (t // tps, 0, 0, t % tps)
    tok_major = lambda t: (t // tps, 0, t % tps, 0)
    tok_bf = jax.ShapeDtypeStruct((n_tok, d_model), BF16)
    kern = functools.partial(_inproj_kernel, tiles_per_seq=tps, d_model=d_model)
    return pl.pallas_call(
        kern,
        grid=(n_tok // tm,),
        in_specs=[act] + [_resident(p, layer) for p in params],
        out_specs=[act,
                   pl.BlockSpec((None, N_HEADS, QK_DIM, tm), feat_major),
                   pl.BlockSpec((None, N_HEADS, tm, QK_DIM), tok_major),
                   pl.BlockSpec((None, N_HEADS, V_ROWS, tm), feat_major),
                   act],
        out_shape=[tok_bf,
                   jax.ShapeDtypeStruct((bsz, N_HEADS, QK_DIM, seq), BF16),
                   jax.ShapeDtypeStruct((bsz, N_HEADS, seq, QK_DIM), BF16),
                   jax.ShapeDtypeStruct((bsz, N_HEADS, V_ROWS, seq), BF16),
                   tok_bf],
        scratch_shapes=[pltpu.VMEM((1, LANES), F32)],
        compiler_params=pltpu.CompilerParams(dimension_semantics=("arbitrary",),
                                             vmem_limit_bytes=VMEM_LIMIT),
        name="inproj_sgu",
    )(x2, *params)


def _pair_order(n_tiles):
    return ([(u, u) for u in range(n_tiles)]
            + [(i, j) for i in range(1, n_tiles) for j in range(i)])


def _attn_kernel(qt_ref, k_ref, vt_ref, o_ref, *scratch, n_tiles):
    t = ATTN_TILE
    m_all, acc_all = scratch

    def tile(idx):
        return pl.ds(idx * t, t)

    def blocks(masked):
        if not masked:
            return [(0, t, 0, t // 2), (0, t, t // 2, t)]
        w = t // DIAG_SPLIT
        return [(0, (c + 1) * w, c * w, (c + 1) * w) for c in range(DIAG_SPLIT)]

    def scores(pair, masked):
        i, j = pair
        out = []
        for k0, k1, q0, q1 in blocks(masked):
            s = jnp.dot(k_ref[pl.ds(j * t + k0, k1 - k0), :], qt_ref[:, pl.ds(i * t + q0, q1 - q0)],
                        preferred_element_type=F32)
            if masked:
                key = lax.broadcasted_iota(jnp.int32, s.shape, 0) + k0
                qry = lax.broadcasted_iota(jnp.int32, s.shape, 1) + q0
                s = jnp.where(key <= qry, s, NEG_BIG)
            out.append((s, jnp.max(s, axis=0, keepdims=True)))
        return masked, out

    def softmax(pair, masked, scored_blocks):
        i, _ = pair
        out = []
        for (k0, k1, q0, q1), (s, col_max) in zip(blocks(masked), scored_blocks):
            m_old = m_all[i, :, q0:q1]
            m_new = jnp.maximum(m_old, col_max)
            m_all[i, :, q0:q1] = m_new
            out.append((jnp.exp2(s - m_new).astype(BF16), jnp.exp2(m_old - m_new)))
        return masked, out

    def accumulate(pair, masked, prob_blocks):
        i, j = pair
        for (k0, k1, q0, q1), (p, alpha) in zip(blocks(masked), prob_blocks):
            pv = jnp.dot(vt_ref[:, pl.ds(j * t + k0, k1 - k0)], p, preferred_element_type=F32)
            acc_all[i, :, q0:q1] = alpha * acc_all[i, :, q0:q1] + pv

    pairs = _pair_order(n_tiles)
    n_pairs = len(pairs)

    m_all[...] = jnp.full_like(m_all, NEG_BIG)
    acc_all[...] = jnp.zeros_like(acc_all)

    scored = {n: scores(pairs[n], True) for n in range(LAG)}
    probs = {}
    for n in range(n_pairs):
        if n >= LAG:
            accumulate(pairs[n - LAG], *probs.pop(n - LAG))
        if n + LAG < n_pairs:
            scored[n + LAG] = scores(pairs[n + LAG], n + LAG < n_tiles)
        probs[n] = softmax(pairs[n], *scored.pop(n))
    for n in range(n_pairs - LAG, n_pairs):
        accumulate(pairs[n], *probs.pop(n))

    for i in range(n_tiles):
        acc = acc_all[i]
        o_ref[:, tile(i)] = (acc[:HEAD_DIM] * (1.0 / acc[HEAD_DIM:HEAD_DIM + 1])).astype(BF16)


def _attention(qt, k, vt):
    bsz, n_heads, _, seq = qt.shape
    t = ATTN_TILE
    assert seq % t == 0
    n_tiles = seq // t
    per_head = lambda b, h: (b, h, 0, 0)
    kern = functools.partial(_attn_kernel, n_tiles=n_tiles)
    row = pltpu.VMEM((1, t), F32)
    return pl.pallas_call(
        kern,
        grid=(bsz, n_heads),
        in_specs=[pl.BlockSpec((None, None, QK_DIM, seq), per_head),
                  pl.BlockSpec((None, None, seq, QK_DIM), per_head),
                  pl.BlockSpec((None, None, V_ROWS, seq), per_head)],
        out_specs=pl.BlockSpec((None, HEAD_DIM, seq), lambda b, h: (b, h, 0)),
        out_shape=jax.ShapeDtypeStruct((bsz, n_heads * HEAD_DIM, seq), BF16),
        scratch_shapes=[pltpu.VMEM((n_tiles, 1, t), F32),
                        pltpu.VMEM((n_tiles, V_ROWS, t), F32)],
        compiler_params=pltpu.CompilerParams(
            dimension_semantics=("arbitrary", "arbitrary"),
            vmem_limit_bytes=VMEM_LIMIT),
        name="fox_attention",
    )(qt, k, vt)


def _out_ffn_kernel(x_ref, ya_ref, ybt_ref, sgb_ref, wo_ref, gpost_ref, gffn_ref,
                    wg_ref, wu_ref, wd_ref, gffn_post_ref, o_ref):
    tm = x_ref.shape[0]
    groups = [slice(r, r + tm // FFN_GROUPS) for r in range(0, tm, tm // FFN_GROUPS)]
    mix = []
    for rs in groups:
        yb = ybt_ref[:, rs].astype(F32).T
        merged = (ya_ref[rs, :].astype(F32) + sgb_ref[rs, :].astype(F32) * yb).astype(BF16)
        mix.append(jnp.dot(merged, wo_ref[...], preferred_element_type=F32))
    x1, gate, up = [], [], []
    for rs, mx in zip(groups, mix):
        x1.append(x_ref[rs, :] + mx * _rms_scale(mx) * gpost_ref[...])
        h = (x1[-1] * _rms_scale(x1[-1]) * gffn_ref[...]).astype(BF16)
        gate.append(jnp.dot(h, wg_ref[...], preferred_element_type=F32))
        up.append(jnp.dot(h, wu_ref[...], preferred_element_type=F32))
    down = []
    for g, u in zip(gate, up):
        act = (g * _sigmoid(g) * u).astype(BF16)
        down.append(jnp.dot(act, wd_ref[...], preferred_element_type=F32))
    for rs, xr, dn in zip(groups, x1, down):
        o_ref[rs, :] = xr + dn * _rms_scale(dn) * gffn_post_ref[...]


def _out_ffn(x2, ya, ybt, sgb, params, *, layer, seq):
    n_tok, d_model = x2.shape
    tm = FFN_TILE
    tps = seq // tm
    act = pl.BlockSpec((tm, d_model), lambda i: (i, 0))
    feat = pl.BlockSpec((None, d_model, tm), lambda i: (i // tps, 0, i % tps))
    return pl.pallas_call(
        _out_ffn_kernel,
        grid=(n_tok // tm,),
        in_specs=[act, act, feat, act] + [_resident(p, layer) for p in params],
        out_specs=act,
        out_shape=jax.ShapeDtypeStruct((n_tok, d_model), F32),
        compiler_params=pltpu.CompilerParams(dimension_semantics=("arbitrary",),
                                             vmem_limit_bytes=VMEM_LIMIT),
        name="out_ffn",
    )(x2, ya, ybt, sgb, *params)


def kernel(x, mix_pre_g, w_in, b_forget, sgu_norm_g, w_spatial, b_spatial, w_out, mix_post_g, ffn_pre_g, w_gate, w_up, w_down, ffn_post_g):
    bsz, seq, d_model = x.shape
    depth = w_in.shape[0]
    n_main = 7 * d_model
    assert seq % ATTN_TILE == 0 and seq % TOKEN_TILE == 0 and TOKEN_TILE % CHUNK == 0
    assert w_in.shape[2] == n_main + N_HEADS and d_model == N_HEADS * HEAD_DIM
    x2 = x.reshape(bsz * seq, d_model)
    rows = lambda g: g.reshape(depth, 1, -1)
    seg = lambda s: w_in[:, :, s * d_model:(s + 1) * d_model]
    w_main = jnp.concatenate([seg(s) for s in (0, 1, 3, 5, 6)], axis=2).astype(BF16)
    w_qvt = jnp.concatenate([seg(2), seg(4)], axis=2).transpose(0, 2, 1).astype(BF16)
    w_f = jnp.pad(w_in[:, :, n_main:], ((0, 0), (0, 0), (0, LANES - N_HEADS))).astype(BF16)
    b_f = rows(jnp.pad(b_forget, ((0, 0), (0, LANES - N_HEADS))))
    proj_params = (rows(mix_pre_g), w_main, w_qvt, w_f, b_f, rows(sgu_norm_g),
                   w_spatial, b_spatial.transpose(0, 2, 1))
    ffn_params = (w_out.astype(BF16), rows(mix_post_g), rows(ffn_pre_g), w_gate.astype(BF16),
                  w_up.astype(BF16), w_down.astype(BF16), rows(ffn_post_g))
    for l in range(depth):
        ya, qt, k, vt, sgb = _inproj(x2, proj_params, layer=l, bsz=bsz, seq=seq)
        ybt = _attention(qt, k, vt)
        x2 = _out_ffn(x2, ya, ybt, sgb, ffn_params, layer=l, seq=seq)
    return x2.reshape(bsz, seq, d_model)
```
